```python
import jax, jax.numpy as jnp
from jax import lax
import numpy as np

D_MODEL = 4096
BATCH = 1
SEQ = 8192
DEPTH = 2
DEC_BATCH = 1
DEC_SEQ = 16384
PAST_LEN = 128

D_POOL = D_MODEL // 4
D_MLSTM = D_MODEL // 4
D_MLA = D_MODEL // 2
POOL_WINDOWS = (2, 4, 8, 16)
POOL_GROUPS = 4
POOL_GW = D_POOL // POOL_GROUPS
MLSTM_HEADS = 4
MLSTM_HD = D_MLSTM // MLSTM_HEADS
MLSTM_CHUNK = 128
MLSTM_GATES = 4 * MLSTM_HEADS
MLA_HEADS = 16
MLA_V = D_MLA // MLA_HEADS
MLA_NOPE = 128
MLA_ROPE = 64
MLA_QK = MLA_NOPE + MLA_ROPE
Q_LORA = 1536
KV_LORA = 512
ROPE_THETA = 10000.0
Q_BLOCK = 128
NORM_EPS = 1e-6
IN_SIZES = (D_POOL, D_POOL,
            D_MLSTM, D_MLSTM, D_MLSTM, D_MLSTM, D_MLSTM,
            MLSTM_GATES,
            Q_LORA, KV_LORA, MLA_ROPE, D_MLA)
N_IN = 2 * D_POOL + 5 * D_MLSTM + MLSTM_GATES + Q_LORA + KV_LORA + MLA_ROPE + D_MLA

kernel_name = "hybrid_pool_mlstm_mla_encoder"


def _rms_norm(x, g):
    xf = x.astype(jnp.float32)
    y = xf * lax.rsqrt(jnp.mean(xf * xf, axis=-1, keepdims=True) + NORM_EPS)
    return (y * g.astype(jnp.float32)).astype(x.dtype)


def _rope_tables(S):
    pos = jnp.arange(S, dtype=jnp.float32)
    inv_freq = ROPE_THETA ** (-(jnp.arange(0, MLA_ROPE, 2, dtype=jnp.float32) / MLA_ROPE))
    ang = pos[:, None] * inv_freq[None, :]
    return jnp.cos(ang), jnp.sin(ang)


def _rope(x, cos, sin):
    xf = x.astype(jnp.float32)
    half = xf.shape[-1] // 2
    x1, x2 = xf[..., :half], xf[..., half:]
    return jnp.concatenate([x1 * cos - x2 * sin, x2 * cos + x1 * sin], axis=-1).astype(x.dtype)


def _pool_mixer(xp, pool_w, pool_scale):
    B, S, _ = xp.shape
    xf = xp.astype(jnp.float32)
    cs = jnp.concatenate([jnp.zeros((B, 1, D_POOL), jnp.float32), jnp.cumsum(xf, axis=1)], axis=1)
    pos = jnp.arange(S)
    outs = []
    for g, w in enumerate(POOL_WINDOWS):
        lo = jnp.clip(pos - w // 2, 0, S)
        hi = jnp.clip(pos + w // 2, 0, S)
        sl = slice(g * POOL_GW, (g + 1) * POOL_GW)
        csg = cs[..., sl]
        mean = (csg[:, hi] - csg[:, lo]) / (hi - lo).astype(jnp.float32)[None, :, None]
        diff = (mean - xf[..., sl]).astype(xp.dtype)
        outs.append(jnp.einsum('bsc,cd->bsd', diff, pool_w[g]))
    return (jnp.concatenate(outs, axis=-1) * pool_scale).astype(xp.dtype)


def _mlstm_chunk_scan(q, k, v, i_pre, logf):
    B, H, S, d = q.shape
    L = MLSTM_CHUNK
    nc = S // L

    def to_chunks(a):
        return jnp.moveaxis(a.reshape(B, H, nc, L, *a.shape[3:]), 2, 0)

    xs = (to_chunks(q), to_chunks(k), to_chunks(v), to_chunks(i_pre), to_chunks(logf))
    causal = jnp.tril(jnp.ones((L, L), dtype=bool))

    def step(carry, inp):
        C, n, m = carry
        qc, kc, vc, ic, fc = inp
        b = jnp.cumsum(fc, axis=-1)
        D = jnp.where(causal, b[..., :, None] - b[..., None, :] + ic[..., None, :], -jnp.inf)
        inter = b + m[..., None]
        m_t = jnp.maximum(inter, jnp.max(D, axis=-1))
        w = jnp.exp(D - m_t[..., None]) * jnp.einsum('bhtd,bhsd->bhts', qc, kc)
        ei = jnp.exp(inter - m_t)
        num = ei[..., None] * jnp.einsum('bhtd,bhde->bhte', qc, C) + jnp.einsum('bhts,bhse->bhte', w, vc)
        den = ei * jnp.einsum('bhtd,bhd->bht', qc, n) + jnp.sum(w, axis=-1)
        h = num / jnp.maximum(jnp.abs(den), jnp.exp(-m_t))[..., None]
        bL = b[..., -1]
        a = bL[..., None] - b + ic
        m_new = jnp.maximum(bL + m, jnp.max(a, axis=-1))
        decay = jnp.exp(bL + m - m_new)
        ea = jnp.exp(a - m_new[..., None])
        C = decay[..., None, None] * C + jnp.einsum('bhs,bhsd,bhse->bhde', ea, kc, vc)
        n = decay[..., None] * n + jnp.einsum('bhs,bhsd->bhd', ea, kc)
        return (C, n, m_new), h

    init = (jnp.zeros((B, H, d, d), jnp.float32), jnp.zeros((B, H, d), jnp.float32),
            jnp.zeros((B, H), jnp.float32))
    _, hs = lax.scan(step, init, xs)
    return jnp.moveaxis(hs, 0, 2).reshape(B, H, S, d)


def _mlstm_mixer(q, k, v, o, gates, gate_bias, norm_g):
    B, S, _ = q.shape

    def heads(a):
        return a.astype(jnp.float32).reshape(B, S, MLSTM_HEADS, MLSTM_HD).transpose(0, 2, 1, 3)

    qh, kh, vh = heads(q), heads(k) * (MLSTM_HD ** -0.5), heads(v)
    g = (gates.astype(jnp.float32) + gate_bias.astype(jnp.float32)).reshape(B, S, 4, MLSTM_HEADS)
    g = g.transpose(2, 0, 3, 1)
    h_fwd = _mlstm_chunk_scan(qh, kh, vh, g[0], jax.nn.log_sigmoid(g[1]))
    flip = lambda a: jnp.flip(a, axis=2)
    h_bwd = flip(_mlstm_chunk_scan(flip(qh), flip(kh), flip(vh), flip(g[2]), flip(jax.nn.log_sigmoid(g[3]))))
    h = (h_fwd + h_bwd).transpose(0, 2, 1, 3)
    h = _rms_norm(h, norm_g.reshape(MLSTM_HEADS, MLSTM_HD)).reshape(B, S, D_MLSTM)
    return (jax.nn.sigmoid(o.astype(jnp.float32)) * h).astype(q.dtype)


def _mla_mixer(q_lat, kv_lat, k_rope, qlat_g, w_uq, kvlat_g, w_ukv, qn_g, qr_g, kn_g, kr_g, cos, sin):
    B, S, _ = q_lat.shape
    q = (_rms_norm(q_lat, qlat_g) @ w_uq).reshape(B, S, MLA_HEADS, MLA_QK)
    kv = (_rms_norm(kv_lat, kvlat_g) @ w_ukv).reshape(B, S, MLA_HEADS, MLA_NOPE + MLA_V)
    cq, sq = cos[None, :, None, :], sin[None, :, None, :]
    q_nope = _rms_norm(q[..., :MLA_NOPE], qn_g)
    q_pe = _rope(_rms_norm(q[..., MLA_NOPE:], qr_g), cq, sq)
    k_nope = _rms_norm(kv[..., :MLA_NOPE], kn_g)
    v = kv[..., MLA_NOPE:]
    k_pe = _rope(_rms_norm(k_rope, kr_g), cos[None], sin[None])
    k_pe = jnp.broadcast_to(k_pe[:, :, None, :], (B, S, MLA_HEADS, MLA_ROPE))
    qf = jnp.concatenate([q_nope, q_pe], axis=-1)
    kt = jnp.concatenate([k_nope, k_pe], axis=-1).transpose(0, 2, 1, 3)
    vt = v.transpose(0, 2, 1, 3)
    nb = S // Q_BLOCK
    qb = qf.reshape(B, nb, Q_BLOCK, MLA_HEADS, MLA_QK).transpose(1, 0, 3, 2, 4)
    scale = MLA_QK ** -0.5

    def attend(qblk):
        s = jnp.einsum('bhqd,bhkd->bhqk', qblk, kt, preferred_element_type=jnp.float32) * scale
        p = jax.nn.softmax(s, axis=-1)
        return jnp.einsum('bhqk,bhkd->bhqd', p.astype(vt.dtype), vt)

    out = lax.map(attend, qb)
    return out.transpose(1, 0, 3, 2, 4).reshape(B, S, D_MLA)


def _layer(x, cos, sin, norm_g, w_in, gate_bias, pool_w, pool_scale, mlstm_norm_g,
           qlat_g, w_uq, kvlat_g, w_ukv, qn_g, qr_g, kn_g, kr_g, w_out):
    h = _rms_norm(x, norm_g)
    u = h @ w_in
    idx = np.cumsum(IN_SIZES)[:-1].tolist()
    (p_x, p_z, m_q, m_k, m_v, m_o, m_z, m_g, a_qlat, a_kvlat, a_krope, a_z) = jnp.split(u, idx, axis=-1)
    pool_out = _pool_mixer(p_x, pool_w, pool_scale) * jax.nn.silu(p_z)
    mlstm_out = _mlstm_mixer(m_q, m_k, m_v, m_o, m_g, gate_bias, mlstm_norm_g) * jax.nn.silu(m_z)
    mla_out = _mla_mixer(a_qlat, a_kvlat, a_krope, qlat_g, w_uq, kvlat_g, w_ukv,
                         qn_g, qr_g, kn_g, kr_g, cos, sin) * jax.nn.silu(a_z)
    mix = jnp.concatenate([pool_out.astype(x.dtype), mlstm_out.astype(x.dtype), mla_out.astype(x.dtype)], axis=-1)
    return x + mix @ w_out


def _trunk(x, params):
    cos, sin = _rope_tables(x.shape[1])
    for l in range(DEPTH):
        x = _layer(x, cos, sin, *[p[l] for p in params])
    return x


def setup_inputs(seed: int = 0) -> dict:
    key = jax.random.key(seed)
    ks = jax.random.split(key, 20)
    f32 = jnp.float32
    nrm = lambda k, shape, s: s * jax.random.normal(k, shape, f32)
    gain = lambda k, shape: 1.0 + 0.02 * jax.random.normal(k, shape, f32)
    gate_base = jnp.array([0.0, 3.0, 0.0, 3.0], f32)[None, :, None]
    gate_bias = (gate_base + nrm(ks[4], (DEPTH, 4, MLSTM_HEADS), 0.5)).reshape(DEPTH, MLSTM_GATES)
    return {
        "x_prompt": nrm(ks[0], (BATCH, SEQ, D_MODEL), 1.0),
        "x_sample": nrm(ks[1], (DEC_BATCH, DEC_SEQ, D_MODEL), 1.0),
        "norm_g": gain(ks[2], (DEPTH, D_MODEL)),
        "w_in": nrm(ks[3], (DEPTH, D_MODEL, N_IN), D_MODEL ** -0.5),
        "gate_bias": gate_bias,
        "pool_w": nrm(ks[5], (DEPTH, POOL_GROUPS, POOL_GW, POOL_GW), POOL_GW ** -0.5),
        "pool_scale": gain(ks[6], (DEPTH, D_POOL)),
        "mlstm_norm_g": gain(ks[7], (DEPTH, D_MLSTM)),
        "qlat_g": gain(ks[8], (DEPTH, Q_LORA)),
        "w_uq": nrm(ks[9], (DEPTH, Q_LORA, MLA_HEADS * MLA_QK), Q_LORA ** -0.5),
        "kvlat_g": gain(ks[10], (DEPTH, KV_LORA)),
        "w_ukv": nrm(ks[11], (DEPTH, KV_LORA, MLA_HEADS * (MLA_NOPE + MLA_V)), KV_LORA ** -0.5),
        "qn_g": gain(ks[12], (DEPTH, MLA_NOPE)),
        "qr_g": gain(ks[13], (DEPTH, MLA_ROPE)),
        "kn_g": gain(ks[14], (DEPTH, MLA_NOPE)),
        "kr_g": gain(ks[15], (DEPTH, MLA_ROPE)),
        "w_out": nrm(ks[16], (DEPTH, D_MODEL, D_MODEL), D_MODEL ** -0.5),
    }


def reference(x_prompt, x_sample, norm_g, w_in, gate_bias, pool_w, pool_scale, mlstm_norm_g,
              qlat_g, w_uq, kvlat_g, w_ukv, qn_g, qr_g, kn_g, kr_g, w_out):
    params = (norm_g, w_in, gate_bias, pool_w, pool_scale, mlstm_norm_g,
              qlat_g, w_uq, kvlat_g, w_ukv, qn_g, qr_g, kn_g, kr_g, w_out)
    y_prompt = _trunk(x_prompt, params)
    y_sample = _trunk(x_sample, params)
    return (y_prompt, y_sample)
```

```python
import functools

import numpy as np
import jax
import jax.numpy as jnp
from jax import lax
from jax.experimental import pallas as pl
from jax.experimental.pallas import tpu as pltpu

D_MODEL = 4096
D_POOL = 1024
D_MLSTM = 1024
D_MLA = 2048
POOL_WINDOWS = (2, 4, 8, 16)
POOL_GW = 256
MLSTM_HEADS = 4
MLSTM_HD = 256
MLSTM_CHUNK = 128
MLSTM_GATES = 16
MLA_HEADS = 16
MLA_V = 128
MLA_NOPE = 128
MLA_ROPE = 64
MLA_QK = 192
Q_LORA = 1536
KV_LORA = 512
ROPE_THETA = 10000.0
NORM_EPS = 1e-6
N_IN = 11344

LANES = 128
QK_PAD = 256
VMEM_LIMIT = 56 * 1024 * 1024

OFF_QLAT = 0
OFF_KVLAT = 1536
OFF_AZ = 2048
OFF_PX = 4096
OFF_PZ = 5120
OFF_MQ = 6144
OFF_MK = 7168
OFF_MV = 8192
OFF_MO = 9216
OFF_MZ = 10240
OFF_MISC = 11264
N_IN_PAD = 11520
MISC_GATE0 = MLA_ROPE

NEG_BIG = -1e30

F32 = jnp.float32
BF16 = jnp.bfloat16


def _cparams(sem):
    return pltpu.CompilerParams(dimension_semantics=sem, vmem_limit_bytes=VMEM_LIMIT)


def _sigmoid(x):
    return 1.0 / (1.0 + jnp.exp(-x))


def _silu(x):
    return x * _sigmoid(x)


def _dot(a, b):
    return jnp.dot(a, b, preferred_element_type=F32)


def _dot_nt(a, b):
    return lax.dot_general(a, b, (((1,), (1,)), ((), ())), preferred_element_type=F32)


def _dot_tn(a, b):
    return lax.dot_general(a, b, (((0,), (0,)), ((), ())), preferred_element_type=F32)


def _norm_matmul_kernel(x_ref, g_ref, w_ref, o_ref, h_ref):
    @pl.when(pl.program_id(1) == 0)
    def _():
        x = x_ref[...]
        ms = jnp.mean(x * x, axis=-1, keepdims=True)
        h_ref[...] = (x * lax.rsqrt(ms + NORM_EPS) * g_ref[...]).astype(BF16)

    o_ref[...] = _dot(h_ref[...], w_ref[...]).astype(o_ref.dtype)


def _in_proj(x, g, w, tm, tn):
    S, K = x.shape
    N = w.shape[1]
    return pl.pallas_call(
        _norm_matmul_kernel,
        grid=(S // tm, N // tn),
        in_specs=[
            pl.BlockSpec((tm, K), lambda i, j: (i, 0)),
            pl.BlockSpec((1, K), lambda i, j: (0, 0)),
            pl.BlockSpec((K, tn), lambda i, j: (0, j)),
        ],
        out_specs=pl.BlockSpec((tm, tn), lambda i, j: (i, j)),
        out_shape=jax.ShapeDtypeStruct((S, N), F32),
        scratch_shapes=[pltpu.VMEM((tm, K), BF16)],
        compiler_params=_cparams(("parallel", "arbitrary")),
        name="in_proj",
    )(x, g, w)


POOL_HALO = 8


def _pool_kernel(prev_ref, cur_ref, next_ref, z_ref, w_ref, scale_ref, o_ref, *, seq_len):
    i = pl.program_id(0)
    tm = cur_ref.shape[0]
    rows = tm + 2 * POOL_HALO
    prev = jnp.where(i > 0, prev_ref[...], 0.0)
    nxt = jnp.where(i < pl.num_programs(0) - 1, next_ref[...], 0.0)
    cur = cur_ref[...]
    xall = jnp.concatenate([prev, cur, nxt], axis=0)
    pos = i * tm + lax.broadcasted_iota(jnp.int32, (tm, 1), 0)
    for g, win in enumerate(POOL_WINDOWS):
        half = win // 2
        sl = slice(g * POOL_GW, (g + 1) * POOL_GW)
        acc = xall[:, sl]
        span = 1
        while span < win:
            acc = acc + pltpu.roll(acc, span, axis=0)
            span *= 2
        shift = half - 1
        if shift:
            acc = pltpu.roll(acc, rows - shift, axis=0)
        wsum = acc[POOL_HALO:POOL_HALO + tm]
        cnt = jnp.minimum(pos + half, seq_len) - jnp.maximum(pos - half, 0)
        diff = wsum / cnt.astype(F32) - cur[:, sl]
        y = _dot(diff.astype(BF16), w_ref[g])
        o_ref[:, sl] = (y * scale_ref[:, sl] * _silu(z_ref[:, sl])).astype(o_ref.dtype)


def _pool(u, pool_w, pool_scale, tm):
    S = u.shape[0]
    hb = tm // POOL_HALO
    n_halo = S // POOL_HALO
    cpx = OFF_PX // D_POOL
    return pl.pallas_call(
        functools.partial(_pool_kernel, seq_len=S),
        grid=(S // tm,),
        in_specs=[
            pl.BlockSpec((POOL_HALO, D_POOL), lambda i: (jnp.maximum(i * hb - 1, 0), cpx)),
            pl.BlockSpec((tm, D_POOL), lambda i: (i, cpx)),
            pl.BlockSpec((POOL_HALO, D_POOL), lambda i: (jnp.minimum((i + 1) * hb, n_halo - 1), cpx)),
            pl.BlockSpec((tm, D_POOL), lambda i: (i, OFF_PZ // D_POOL)),
            pl.BlockSpec((len(POOL_WINDOWS), POOL_GW, POOL_GW), lambda i: (0, 0, 0)),
            pl.BlockSpec((1, D_POOL), lambda i: (0, 0)),
        ],
        out_specs=pl.BlockSpec((tm, D_POOL), lambda i: (i, 0)),
        out_shape=jax.ShapeDtypeStruct((S, D_POOL), BF16),
        compiler_params=_cparams(("parallel",)),
        name="pool",
    )(u, u, u, u, pool_w, pool_scale)


def _split3(x):
    hi = x.astype(BF16)
    r1 = x - hi.astype(F32)
    mid = r1.astype(BF16)
    lo = (r1 - mid.astype(F32)).astype(BF16)
    return hi, mid, lo


def _log_sigmoid(x):
    return jnp.minimum(x, 0.0) - jnp.log(1.0 + jnp.exp(-jnp.abs(x)))


def _mlstm_dir_head(q, k, v, g, gt, c_ref, n_ref, m_ref, idx, reverse, lane_i, lane_f):
    L = q.shape[0]
    row = lax.broadcasted_iota(jnp.int32, (L, L), 0)
    col = lax.broadcasted_iota(jnp.int32, (L, L), 1)
    keep = (col >= row) if reverse else (col <= row)
    tri = jnp.where(keep, 1.0, 0.0).astype(BF16)

    i_row = gt[lane_i:lane_i + 1, :]
    i_col = g[:, lane_i:lane_i + 1]
    f_row = _log_sigmoid(gt[lane_f:lane_f + 1, :])
    f_col = _log_sigmoid(g[:, lane_f:lane_f + 1])
    b_row = sum(_dot_nt(p, tri) for p in _split3(jnp.broadcast_to(f_row, (8, L))))[0:1]
    b_col = sum(_dot(tri, p) for p in _split3(jnp.broadcast_to(f_col, (L, LANES))))[:, 0:1]
    b_last = b_row[:, 0:1] if reverse else b_row[:, L - 1:L]

    m = m_ref[idx][:, 0:1]
    dmat = jnp.where(keep, b_col - b_row + i_row, NEG_BIG)
    inter = b_col + m
    m_t = jnp.maximum(inter, jnp.max(dmat, axis=-1, keepdims=True))
    qb = q.astype(BF16)
    kb = (k * (MLSTM_HD ** -0.5)).astype(BF16)
    vb = v.astype(BF16)
    w = jnp.exp(dmat - m_t) * _dot_nt(qb, kb)
    ei = jnp.exp(inter - m_t)
    c_old = c_ref[idx]
    n_old = n_ref[idx]
    num = ei * _dot(qb, c_old.astype(BF16)) + _dot(w.astype(BF16), vb)
    den = ei * jnp.sum(q * n_old, axis=-1, keepdims=True) + jnp.sum(w, axis=-1, keepdims=True)
    h = num / jnp.maximum(jnp.abs(den), jnp.exp(-m_t))

    a_col = b_last - b_col + i_col
    a_row = b_last - b_row + i_row
    m_new = jnp.maximum(b_last + m, jnp.max(a_row, axis=-1, keepdims=True))
    decay = jnp.exp(b_last + m - m_new)
    ek = jnp.exp(a_col - m_new) * (k * (MLSTM_HD ** -0.5))
    c_ref[idx] = decay * c_old + _dot_tn(ek.astype(BF16), vb)
    n_ref[idx] = decay * n_old + jnp.sum(ek, axis=0, keepdims=True)
    m_ref[idx] = jnp.broadcast_to(m_new, (1, LANES))
    return h


def _mlstm_kernel(qf_ref, kf_ref, vf_ref, gf_ref, qb_ref, kb_ref, vb_ref, gb_ref, bias_ref,
                  hf_ref, hb_ref, c_ref, n_ref, m_ref):
    @pl.when(pl.program_id(0) == 0)
    def _():
        c_ref[...] = jnp.zeros_like(c_ref)
        n_ref[...] = jnp.zeros_like(n_ref)
        m_ref[...] = jnp.zeros_like(m_ref)

    dirs = ((qf_ref, kf_ref, vf_ref, gf_ref, hf_ref, False), (qb_ref, kb_ref, vb_ref, gb_ref, hb_ref, True))
    for d, (q_ref, k_ref, v_ref, g_ref, h_ref, reverse) in enumerate(dirs):
        g = g_ref[...] + bias_ref[...]
        gt = g.T
        for hd in range(MLSTM_HEADS):
            sl = slice(hd * MLSTM_HD, (hd + 1) * MLSTM_HD)
            lane_i = MISC_GATE0 + (2 * d) * MLSTM_HEADS + hd
            lane_f = MISC_GATE0 + (2 * d + 1) * MLSTM_HEADS + hd
            h_ref[:, sl] = _mlstm_dir_head(q_ref[:, sl], k_ref[:, sl], v_ref[:, sl], g, gt,
                                           c_ref, n_ref, m_ref, d * MLSTM_HEADS + hd, reverse, lane_i, lane_f)


def _mlstm(u, bias_misc):
    S = u.shape[0]
    L = MLSTM_CHUNK
    nc = S // L
    fwd = lambda c: c
    bwd = lambda c: nc - 1 - c

    def spec(off, width, cmap):
        return pl.BlockSpec((L, width), lambda c: (cmap(c), off // width))

    in_specs = []
    for cmap in (fwd, bwd):
        in_specs += [spec(OFF_MQ, D_MLSTM, cmap), spec(OFF_MK, D_MLSTM, cmap), spec(OFF_MV, D_MLSTM, cmap),
                     spec(OFF_MISC, LANES, cmap)]
    in_specs.append(pl.BlockSpec((1, LANES), lambda c: (0, 0)))
    nstate = 2 * MLSTM_HEADS
    return pl.pallas_call(
        _mlstm_kernel,
        grid=(nc,),
        in_specs=in_specs,
        out_specs=[pl.BlockSpec((L, D_MLSTM), lambda c: (c, 0)),
                   pl.BlockSpec((L, D_MLSTM), lambda c: (nc - 1 - c, 0))],
        out_shape=[jax.ShapeDtypeStruct((S, D_MLSTM), F32)] * 2,
        scratch_shapes=[pltpu.VMEM((nstate, MLSTM_HD, MLSTM_HD), F32),
                        pltpu.VMEM((nstate, 1, MLSTM_HD), F32),
                        pltpu.VMEM((nstate, 1, LANES), F32)],
        compiler_params=_cparams(("arbitrary",)),
        name="mlstm",
    )(u, u, u, u, u, u, u, u, bias_misc)


def _mlstm_out_kernel(hf_ref, hb_ref, o_ref, z_ref, g_ref, out_ref):
    for hd in range(MLSTM_HEADS):
        sl = slice(hd * MLSTM_HD, (hd + 1) * MLSTM_HD)
        h = hf_ref[:, sl] + hb_ref[:, sl]
        ms = jnp.mean(h * h, axis=-1, keepdims=True)
        hn = h * lax.rsqrt(ms + NORM_EPS) * g_ref[:, sl]
        out_ref[:, sl] = (_sigmoid(o_ref[:, sl]) * hn * _silu(z_ref[:, sl])).astype(out_ref.dtype)


def _mlstm_out(hf, hb, u, norm_g, tm):
    S = hf.shape[0]
    return pl.pallas_call(
        _mlstm_out_kernel,
        grid=(S // tm,),
        in_specs=[
            pl.BlockSpec((tm, D_MLSTM), lambda i: (i, 0)),
            pl.BlockSpec((tm, D_MLSTM), lambda i: (i, 0)),
            pl.BlockSpec((tm, D_MLSTM), lambda i: (i, OFF_MO // D_MLSTM)),
            pl.BlockSpec((tm, D_MLSTM), lambda i: (i, OFF_MZ // D_MLSTM)),
            pl.BlockSpec((1, D_MLSTM), lambda i: (0, 0)),
        ],
        out_specs=pl.BlockSpec((tm, D_MLSTM), lambda i: (i, 0)),
        out_shape=jax.ShapeDtypeStruct((S, D_MLSTM), BF16),
        compiler_params=_cparams(("parallel",)),
        name="mlstm_out",
    )(hf, hb, u, u, norm_g)


def _rope128(x, g, cos_ref, sin_ref):
    lane = lax.broadcasted_iota(jnp.int32, x.shape, 1)
    x = jnp.where(lane < MLA_ROPE, x, 0.0)
    ms = jnp.sum(x * x, axis=-1, keepdims=True) * (1.0 / MLA_ROPE)
    xn = x * lax.rsqrt(ms + NORM_EPS) * g
    half = MLA_ROPE // 2
    partner = jnp.where(lane < half, pltpu.roll(xn, LANES - half, axis=1), pltpu.roll(xn, half, axis=1))
    return xn * cos_ref[...] + partner * sin_ref[...]


def _q_up_kernel(x_ref, g_ref, w_ref, qn_ref, qr_ref, cos_ref, sin_ref, o_ref, h_ref, *, heads_per_tile):
    @pl.when(pl.program_id(1) == 0)
    def _():
        x = x_ref[...]
        ms = jnp.mean(x * x, axis=-1, keepdims=True)
        h_ref[...] = (x * lax.rsqrt(ms + NORM_EPS) * g_ref[...]).astype(BF16)

    acc = _dot(h_ref[...], w_ref[...])
    scale = MLA_QK ** -0.5
    for hh in range(heads_per_tile):
        nope = acc[:, hh * QK_PAD: hh * QK_PAD + MLA_NOPE]
        ms = jnp.mean(nope * nope, axis=-1, keepdims=True)
        nope = nope * lax.rsqrt(ms + NORM_EPS) * qn_ref[...]
        pe = _rope128(acc[:, hh * QK_PAD + MLA_NOPE: (hh + 1) * QK_PAD], qr_ref[...], cos_ref, sin_ref)
        o_ref[hh * QK_PAD: hh * QK_PAD + MLA_NOPE, :] = (nope * scale).T.astype(o_ref.dtype)
        o_ref[hh * QK_PAD + MLA_NOPE: (hh + 1) * QK_PAD, :] = (pe * scale).T.astype(o_ref.dtype)


def _q_up(u, g, w, qn_g, qr_g, cos_t, sin_t, tm, heads_per_tile):
    S = u.shape[0]
    tn = heads_per_tile * QK_PAD
    N = MLA_HEADS * QK_PAD
    return pl.pallas_call(
        functools.partial(_q_up_kernel, heads_per_tile=heads_per_tile),
        grid=(S // tm, N // tn),
        in_specs=[
            pl.BlockSpec((tm, Q_LORA), lambda i, j: (i, OFF_QLAT // Q_LORA)),
            pl.BlockSpec((1, Q_LORA), lambda i, j: (0, 0)),
            pl.BlockSpec((Q_LORA, tn), lambda i, j: (0, j)),
            pl.BlockSpec((1, LANES), lambda i, j: (0, 0)),
            pl.BlockSpec((1, LANES), lambda i, j: (0, 0)),
            pl.BlockSpec((tm, LANES), lambda i, j: (i, 0)),
            pl.BlockSpec((tm, LANES), lambda i, j: (i, 0)),
        ],
        out_specs=pl.BlockSpec((tn, tm), lambda i, j: (j, i)),
        out_shape=jax.ShapeDtypeStruct((N, S), BF16),
        scratch_shapes=[pltpu.VMEM((tm, Q_LORA), BF16)],
        compiler_params=_cparams(("parallel", "arbitrary")),
        name="q_up",
    )(u, g, w, qn_g, qr_g, cos_t, sin_t)


def _kv_up_kernel(x_ref, misc_ref, g_ref, w_ref, kn_ref, kr_ref, cos_ref, sin_ref, k_ref, vt_ref):
    x = x_ref[...]
    ms = jnp.mean(x * x, axis=-1, keepdims=True)
    h = (x * lax.rsqrt(ms + NORM_EPS) * g_ref[...]).astype(BF16)
    acc = _dot(h, w_ref[...])
    pe = _rope128(misc_ref[...], kr_ref[...], cos_ref, sin_ref).astype(k_ref.dtype)
    for hd in range(MLA_HEADS):
        nope = acc[:, hd * MLA_NOPE: (hd + 1) * MLA_NOPE]
        ms = jnp.mean(nope * nope, axis=-1, keepdims=True)
        k_ref[:, hd * QK_PAD: hd * QK_PAD + MLA_NOPE] = (nope * lax.rsqrt(ms + NORM_EPS) * kn_ref[...]).astype(k_ref.dtype)
        k_ref[:, hd * QK_PAD + MLA_NOPE: (hd + 1) * QK_PAD] = pe
    vt_ref[...] = acc[:, MLA_HEADS * MLA_NOPE:].T.astype(vt_ref.dtype)


def _kv_up(u, g, w, kn_g, kr_g, cos_t, sin_t, tk):
    S = u.shape[0]
    return pl.pallas_call(
        _kv_up_kernel,
        grid=(S // tk,),
        in_specs=[
            pl.BlockSpec((tk, KV_LORA), lambda i: (i, OFF_KVLAT // KV_LORA)),
            pl.BlockSpec((tk, LANES), lambda i: (i, OFF_MISC // LANES)),
            pl.BlockSpec((1, KV_LORA), lambda i: (0, 0)),
            pl.BlockSpec((KV_LORA, 2 * D_MLA), lambda i: (0, 0)),
            pl.BlockSpec((1, LANES), lambda i: (0, 0)),
            pl.BlockSpec((1, LANES), lambda i: (0, 0)),
            pl.BlockSpec((tk, LANES), lambda i: (i, 0)),
            pl.BlockSpec((tk, LANES), lambda i: (i, 0)),
        ],
        out_specs=[pl.BlockSpec((tk, MLA_HEADS * QK_PAD), lambda i: (i, 0)),
                   pl.BlockSpec((None, D_MLA, tk), lambda i: (i, 0, 0))],
        out_shape=[jax.ShapeDtypeStruct((S, MLA_HEADS * QK_PAD), BF16),
                   jax.ShapeDtypeStruct((S // tk, D_MLA, tk), BF16)],
        compiler_params=_cparams(("parallel",)),
        name="kv_up",
    )(u, u, g, w, kn_g, kr_g, cos_t, sin_t)


def _attn_kernel(qt_ref, k_ref, vt_ref, z_ref, o_ref, acc_ref):
    n_chunks, _, tk = vt_ref.shape
    tq = qt_ref.shape[1]
    qt = qt_ref[...]
    acc_ref[...] = jnp.zeros_like(acc_ref)

    def body(j, carry):
        m, l = carry
        k = k_ref[pl.ds(pl.multiple_of(j * tk, tk), tk), :]
        st = _dot(k, qt)
        m_new = jnp.maximum(m, jnp.max(st, axis=0, keepdims=True))
        alpha = jnp.exp(m - m_new)
        p = jnp.exp(st - m_new)
        l = alpha * l + jnp.sum(p, axis=0, keepdims=True)
        acc_ref[...] = alpha * acc_ref[...] + _dot(vt_ref[j], p.astype(BF16))
        return m_new, l

    m0 = jnp.full((1, tq), NEG_BIG, F32)
    l0 = jnp.zeros((1, tq), F32)
    _, l = lax.fori_loop(0, n_chunks, body, (m0, l0))
    o = (acc_ref[...] / l).T
    o_ref[...] = (o * _silu(z_ref[...])).astype(o_ref.dtype)


def _attn(qt, kf, vt, u, tq):
    S = kf.shape[0]
    n_chunks, _, tk = vt.shape
    return pl.pallas_call(
        _attn_kernel,
        grid=(MLA_HEADS, S // tq),
        in_specs=[
            pl.BlockSpec((QK_PAD, tq), lambda h, i: (h, i)),
            pl.BlockSpec((S, QK_PAD), lambda h, i: (0, h)),
            pl.BlockSpec((n_chunks, MLA_V, tk), lambda h, i: (0, h, 0)),
            pl.BlockSpec((tq, MLA_V), lambda h, i: (i, OFF_AZ // MLA_V + h)),
        ],
        out_specs=pl.BlockSpec((tq, MLA_V), lambda h, i: (i, h)),
        out_shape=jax.ShapeDtypeStruct((S, D_MLA), BF16),
        scratch_shapes=[pltpu.VMEM((MLA_V, tq), F32)],
        compiler_params=_cparams(("parallel", "arbitrary")),
        name="attn",
    )(qt, kf, vt, u)


def _out_proj_kernel(p_ref, m_ref, a_ref, w_ref, x_ref, o_ref):
    acc = _dot(p_ref[...], w_ref[0:D_POOL, :])
    acc += _dot(m_ref[...], w_ref[D_POOL:D_POOL + D_MLSTM, :])
    acc += _dot(a_ref[...], w_ref[D_POOL + D_MLSTM:, :])
    o_ref[...] = x_ref[...] + acc


def _out_proj(pool_o, mlstm_o, mla_o, w, x, tm, tn):
    S = x.shape[0]
    return pl.pallas_call(
        _out_proj_kernel,
        grid=(S // tm, D_MODEL // tn),
        in_specs=[
            pl.BlockSpec((tm, D_POOL), lambda i, j: (i, 0)),
            pl.BlockSpec((tm, D_MLSTM), lambda i, j: (i, 0)),
            pl.BlockSpec((tm, D_MLA), lambda i, j: (i, 0)),
            pl.BlockSpec((D_MODEL, tn), lambda i, j: (0, j)),
            pl.BlockSpec((tm, tn), lambda i, j: (i, j)),
        ],
        out_specs=pl.BlockSpec((tm, tn), lambda i, j: (i, j)),
        out_shape=jax.ShapeDtypeStruct((S, D_MODEL), F32),
        compiler_params=_cparams(("parallel", "arbitrary")),
        name="out_proj",
    )(pool_o, mlstm_o, mla_o, w, x)


def _pad_lanes(v, offset=0):
    return jnp.zeros((1, LANES), F32).at[0, offset:offset + v.shape[0]].set(v.astype(F32))


def _prep_layer(norm_g, w_in, gate_bias, pool_w, pool_scale, mlstm_norm_g,
                qlat_g, w_uq, kvlat_g, w_ukv, qn_g, qr_g, kn_g, kr_g, w_out):
    sizes = (D_POOL, D_POOL, D_MLSTM, D_MLSTM, D_MLSTM, D_MLSTM, D_MLSTM, MLSTM_GATES,
             Q_LORA, KV_LORA, MLA_ROPE, D_MLA)
    offs = np.concatenate([[0], np.cumsum(sizes)])
    (p_x, p_z, m_q, m_k, m_v, m_o, m_z, m_g, a_qlat, a_kvlat, a_krope, a_z) = [
        w_in[:, int(offs[i]):int(offs[i + 1])] for i in range(len(sizes))]
    zeros = lambda n: jnp.zeros((D_MODEL, n), w_in.dtype)
    w_in_p = jnp.concatenate(
        [a_qlat, a_kvlat, a_z, p_x, p_z, m_q, m_k, m_v, m_o, m_z, a_krope, m_g,
         zeros(LANES - MLA_ROPE - MLSTM_GATES), zeros(N_IN_PAD - OFF_MISC - LANES)], axis=1).astype(BF16)

    wq = w_uq.reshape(Q_LORA, MLA_HEADS, MLA_QK)
    wq = jnp.concatenate([wq, jnp.zeros((Q_LORA, MLA_HEADS, QK_PAD - MLA_QK), w_uq.dtype)], axis=-1)
    w_uq_p = wq.reshape(Q_LORA, MLA_HEADS * QK_PAD).astype(BF16)

    wkv = w_ukv.reshape(KV_LORA, MLA_HEADS, MLA_NOPE + MLA_V)
    w_ukv_p = jnp.concatenate([wkv[:, :, :MLA_NOPE].reshape(KV_LORA, -1),
                               wkv[:, :, MLA_NOPE:].reshape(KV_LORA, -1)], axis=1).astype(BF16)
    return dict(
        norm_g=norm_g.reshape(1, D_MODEL), w_in=w_in_p,
        bias_misc=_pad_lanes(gate_bias, MISC_GATE0),
        pool_w=pool_w.astype(BF16), pool_scale=pool_scale.reshape(1, D_POOL),
        mlstm_norm_g=mlstm_norm_g.reshape(1, D_MLSTM),
        qlat_g=qlat_g.reshape(1, Q_LORA), w_uq=w_uq_p,
        kvlat_g=kvlat_g.reshape(1, KV_LORA), w_ukv=w_ukv_p,
        qn_g=qn_g.reshape(1, MLA_NOPE), qr_g=_pad_lanes(qr_g),
        kn_g=kn_g.reshape(1, MLA_NOPE), kr_g=_pad_lanes(kr_g),
        w_out=w_out.astype(BF16),
    )


def _rope_tables(S):
    pos = jnp.arange(S, dtype=F32)
    inv_freq = ROPE_THETA ** (-(jnp.arange(0, MLA_ROPE, 2, dtype=F32) / MLA_ROPE))
    ang = pos[:, None] * inv_freq[None, :]
    cos, sin = jnp.cos(ang), jnp.sin(ang)
    pad = jnp.zeros((S, LANES - MLA_ROPE), F32)
    return jnp.concatenate([cos, cos, pad], axis=1), jnp.concatenate([-sin, sin, pad], axis=1)


def _tiles(S):
    t = lambda pref: min(pref, S)
    return dict(tm_in=t(512), tn_in=768, tm_pool=t(512), tm_mo=t(512), tm_q=t(512), q_heads=4,
                tk=t(512), tq=t(512), tm_out=t(512), tn_out=1024)


def _layer(x, p, cos_t, sin_t):
    S = x.shape[0]
    t = _tiles(S)
    u = _in_proj(x, p["norm_g"], p["w_in"], t["tm_in"], t["tn_in"])
    pool_o = _pool(u, p["pool_w"], p["pool_scale"], t["tm_pool"])
    hf, hb = _mlstm(u, p["bias_misc"])
    mlstm_o = _mlstm_out(hf, hb, u, p["mlstm_norm_g"], t["tm_mo"])
    qt = _q_up(u, p["qlat_g"], p["w_uq"], p["qn_g"], p["qr_g"], cos_t, sin_t, t["tm_q"], t["q_heads"])
    kf, vt = _kv_up(u, p["kvlat_g"], p["w_ukv"], p["kn_g"], p["kr_g"], cos_t, sin_t, t["tk"])
    mla_o = _attn(qt, kf, vt, u, t["tq"])
    return _out_proj(pool_o, mlstm_o, mla_o, p["w_out"], x, t["tm_out"], t["tn_out"])


def _trunk(x, layers):
    x = x[0]
    cos_t, sin_t = _rope_tables(x.shape[0])
    for p in layers:
        x = _layer(x, p, cos_t, sin_t)
    return x[None]


def kernel(x_prompt, x_sample, norm_g, w_in, gate_bias, pool_w, pool_scale, mlstm_norm_g,
           qlat_g, w_uq, kvlat_g, w_ukv, qn_g, qr_g, kn_g, kr_g, w_out):
    params = (norm_g, w_in, gate_bias, pool_w, pool_scale, mlstm_norm_g,
              qlat_g, w_uq, kvlat_g, w_ukv, qn_g, qr_g, kn_g, kr_g, w_out)
    layers = [_prep_layer(*[w[l] for w in params]) for l in range(norm_g.shape[0])]
    return (_trunk(x_prompt, layers), _trunk(x_sample, layers))
```

```python
import functools

import numpy as np
import jax
import jax.numpy as jnp
from jax import lax
from jax.experimental import pallas as pl
from jax.experimental.pallas import tpu as pltpu

D_MODEL = 4096
D_POOL = 1024
D_MLSTM = 1024
D_MLA = 2048
POOL_WINDOWS = (2, 4, 8, 16)
POOL_GW = 256
MLSTM_HEADS = 4
MLSTM_HD = 256
MLSTM_CHUNK = 128
MLSTM_GATES = 16
MLA_HEADS = 16
MLA_V = 128
MLA_NOPE = 128
MLA_ROPE = 64
MLA_QK = 192
Q_LORA = 1536
KV_LORA = 512
ROPE_THETA = 10000.0
NORM_EPS = 1e-6
N_IN = 11344

LANES = 128
QK_PAD = 256
V_ONES = 16
V_AUG = MLA_V + V_ONES
LOG2_E = 1.4426950408889634
VMEM_LIMIT = 56 * 1024 * 1024

OFF_QLAT = 0
OFF_KVLAT = 1536
OFF_AZ = 2048
OFF_PX = 4096
OFF_PZ = 5120
OFF_MQ = 6144
OFF_MK = 7168
OFF_MV = 8192
OFF_MO = 9216
OFF_MZ = 10240
OFF_MISC = 11264
N_IN_PAD = 11520
MISC_GATE0 = MLA_ROPE

NEG_BIG = -1e30

F32 = jnp.float32
BF16 = jnp.bfloat16


def _cparams(sem):
    return pltpu.CompilerParams(dimension_semantics=sem, vmem_limit_bytes=VMEM_LIMIT)


def _sigmoid(x):
    return 1.0 / (1.0 + jnp.exp(-x))


def _silu(x):
    return x * _sigmoid(x)


def _dot(a, b):
    return jnp.dot(a, b, preferred_element_type=F32)


def _dot_nt(a, b):
    return lax.dot_general(a, b, (((1,), (1,)), ((), ())), preferred_element_type=F32)


def _dot_tn(a, b):
    return lax.dot_general(a, b, (((0,), (0,)), ((), ())), preferred_element_type=F32)


def _norm_matmul_kernel(x_ref, g_ref, w_ref, o_ref, h_ref):
    @pl.when(pl.program_id(1) == 0)
    def _():
        x = x_ref[...]
        ms = jnp.mean(x * x, axis=-1, keepdims=True)
        h_ref[...] = (x * lax.rsqrt(ms + NORM_EPS) * g_ref[...]).astype(BF16)

    o_ref[...] = _dot(h_ref[...], w_ref[...]).astype(o_ref.dtype)


def _in_proj(x, g, w, tm, tn):
    S, K = x.shape
    N = w.shape[1]
    return pl.pallas_call(
        _norm_matmul_kernel,
        grid=(S // tm, N // tn),
        in_specs=[
            pl.BlockSpec((tm, K), lambda i, j: (i, 0)),
            pl.BlockSpec((1, K), lambda i, j: (0, 0)),
            pl.BlockSpec((K, tn), lambda i, j: (0, j)),
        ],
        out_specs=pl.BlockSpec((tm, tn), lambda i, j: (i, j)),
        out_shape=jax.ShapeDtypeStruct((S, N), F32),
        scratch_shapes=[pltpu.VMEM((tm, K), BF16)],
        compiler_params=_cparams(("parallel", "arbitrary")),
        name="in_proj",
    )(x, g, w)


POOL_HALO = 8


def _pool_kernel(prev_ref, cur_ref, next_ref, z_ref, w_ref, scale_ref, o_ref, *, seq_len):
    i = pl.program_id(0)
    tm = cur_ref.shape[0]
    rows = tm + 2 * POOL_HALO
    prev = jnp.where(i > 0, prev_ref[...], 0.0)
    nxt = jnp.where(i < pl.num_programs(0) - 1, next_ref[...], 0.0)
    cur = cur_ref[...]
    xall = jnp.concatenate([prev, cur, nxt], axis=0)
    pos = i * tm + lax.broadcasted_iota(jnp.int32, (tm, 1), 0)
    for g, win in enumerate(POOL_WINDOWS):
        half = win // 2
        sl = slice(g * POOL_GW, (g + 1) * POOL_GW)
        acc = xall[:, sl]
        span = 1
        while span < win:
            acc = acc + pltpu.roll(acc, span, axis=0)
            span *= 2
        shift = half - 1
        if shift:
            acc = pltpu.roll(acc, rows - shift, axis=0)
        wsum = acc[POOL_HALO:POOL_HALO + tm]
        cnt = jnp.minimum(pos + half, seq_len) - jnp.maximum(pos - half, 0)
        diff = wsum / cnt.astype(F32) - cur[:, sl]
        y = _dot(diff.astype(BF16), w_ref[g])
        o_ref[:, sl] = (y * scale_ref[:, sl] * _silu(z_ref[:, sl])).astype(o_ref.dtype)


def _pool(u, pool_w, pool_scale, tm):
    S = u.shape[0]
    hb = tm // POOL_HALO
    n_halo = S // POOL_HALO
    cpx = OFF_PX // D_POOL
    return pl.pallas_call(
        functools.partial(_pool_kernel, seq_len=S),
        grid=(S // tm,),
        in_specs=[
            pl.BlockSpec((POOL_HALO, D_POOL), lambda i: (jnp.maximum(i * hb - 1, 0), cpx)),
            pl.BlockSpec((tm, D_POOL), lambda i: (i, cpx)),
            pl.BlockSpec((POOL_HALO, D_POOL), lambda i: (jnp.minimum((i + 1) * hb, n_halo - 1), cpx)),
            pl.BlockSpec((tm, D_POOL), lambda i: (i, OFF_PZ // D_POOL)),
            pl.BlockSpec((len(POOL_WINDOWS), POOL_GW, POOL_GW), lambda i: (0, 0, 0)),
            pl.BlockSpec((1, D_POOL), lambda i: (0, 0)),
        ],
        out_specs=pl.BlockSpec((tm, D_POOL), lambda i: (i, 0)),
        out_shape=jax.ShapeDtypeStruct((S, D_POOL), BF16),
        compiler_params=_cparams(("parallel",)),
        name="pool",
    )(u, u, u, u, pool_w, pool_scale)


def _split3(x):
    hi = x.astype(BF16)
    r1 = x - hi.astype(F32)
    mid = r1.astype(BF16)
    lo = (r1 - mid.astype(F32)).astype(BF16)
    return hi, mid, lo


def _log_sigmoid(x):
    return jnp.minimum(x, 0.0) - jnp.log(1.0 + jnp.exp(-jnp.abs(x)))


def _mlstm_dir_head(q, k, v, g, gt, c_ref, n_ref, m_ref, idx, reverse, lane_i, lane_f):
    L = q.shape[0]
    row = lax.broadcasted_iota(jnp.int32, (L, L), 0)
    col = lax.broadcasted_iota(jnp.int32, (L, L), 1)
    keep = (col >= row) if reverse else (col <= row)
    tri = jnp.where(keep, 1.0, 0.0).astype(BF16)

    i_row = gt[lane_i:lane_i + 1, :]
    i_col = g[:, lane_i:lane_i + 1]
    f_row = _log_sigmoid(gt[lane_f:lane_f + 1, :])
    f_col = _log_sigmoid(g[:, lane_f:lane_f + 1])
    b_row = sum(_dot_nt(p, tri) for p in _split3(jnp.broadcast_to(f_row, (8, L))))[0:1]
    b_col = sum(_dot(tri, p) for p in _split3(jnp.broadcast_to(f_col, (L, LANES))))[:, 0:1]
    b_last = b_row[:, 0:1] if reverse else b_row[:, L - 1:L]

    m = m_ref[idx][:, 0:1]
    dmat = jnp.where(keep, b_col - b_row + i_row, NEG_BIG)
    inter = b_col + m
    m_t = jnp.maximum(inter, jnp.max(dmat, axis=-1, keepdims=True))
    qb = q.astype(BF16)
    kb = (k * (MLSTM_HD ** -0.5)).astype(BF16)
    vb = v.astype(BF16)
    w = jnp.exp(dmat - m_t) * _dot_nt(qb, kb)
    ei = jnp.exp(inter - m_t)
    c_old = c_ref[idx]
    n_old = n_ref[idx]
    num = ei * _dot(qb, c_old.astype(BF16)) + _dot(w.astype(BF16), vb)
    den = ei * jnp.sum(q * n_old, axis=-1, keepdims=True) + jnp.sum(w, axis=-1, keepdims=True)
    h = num / jnp.maximum(jnp.abs(den), jnp.exp(-m_t))

    a_col = b_last - b_col + i_col
    a_row = b_last - b_row + i_row
    m_new = jnp.maximum(b_last + m, jnp.max(a_row, axis=-1, keepdims=True))
    decay = jnp.exp(b_last + m - m_new)
    ek = jnp.exp(a_col - m_new) * (k * (MLSTM_HD ** -0.5))
    c_ref[idx] = decay * c_old + _dot_tn(ek.astype(BF16), vb)
    n_ref[idx] = decay * n_old + jnp.sum(ek, axis=0, keepdims=True)
    m_ref[idx] = jnp.broadcast_to(m_new, (1, LANES))
    return h


def _mlstm_kernel(qf_ref, kf_ref, vf_ref, gf_ref, qb_ref, kb_ref, vb_ref, gb_ref, bias_ref,
                  hf_ref, hb_ref, c_ref, n_ref, m_ref):
    @pl.when(pl.program_id(0) == 0)
    def _():
        c_ref[...] = jnp.zeros_like(c_ref)
        n_ref[...] = jnp.zeros_like(n_ref)
        m_ref[...] = jnp.zeros_like(m_ref)

    dirs = ((qf_ref, kf_ref, vf_ref, gf_ref, hf_ref, False), (qb_ref, kb_ref, vb_ref, gb_ref, hb_ref, True))
    for d, (q_ref, k_ref, v_ref, g_ref, h_ref, reverse) in enumerate(dirs):
        g = g_ref[...] + bias_ref[...]
        gt = g.T
        for hd in range(MLSTM_HEADS):
            sl = slice(hd * MLSTM_HD, (hd + 1) * MLSTM_HD)
            lane_i = MISC_GATE0 + (2 * d) * MLSTM_HEADS + hd
            lane_f = MISC_GATE0 + (2 * d + 1) * MLSTM_HEADS + hd
            h_ref[:, sl] = _mlstm_dir_head(q_ref[:, sl], k_ref[:, sl], v_ref[:, sl], g, gt,
                                           c_ref, n_ref, m_ref, d * MLSTM_HEADS + hd, reverse, lane_i, lane_f)


def _mlstm(u, bias_misc):
    S = u.shape[0]
    L = MLSTM_CHUNK
    nc = S // L
    fwd = lambda c: c
    bwd = lambda c: nc - 1 - c

    def spec(off, width, cmap):
        return pl.BlockSpec((L, width), lambda c: (cmap(c), off // width))

    in_specs = []
    for cmap in (fwd, bwd):
        in_specs += [spec(OFF_MQ, D_MLSTM, cmap), spec(OFF_MK, D_MLSTM, cmap), spec(OFF_MV, D_MLSTM, cmap),
                     spec(OFF_MISC, LANES, cmap)]
    in_specs.append(pl.BlockSpec((1, LANES), lambda c: (0, 0)))
    nstate = 2 * MLSTM_HEADS
    return pl.pallas_call(
        _mlstm_kernel,
        grid=(nc,),
        in_specs=in_specs,
        out_specs=[pl.BlockSpec((L, D_MLSTM), lambda c: (c, 0)),
                   pl.BlockSpec((L, D_MLSTM), lambda c: (nc - 1 - c, 0))],
        out_shape=[jax.ShapeDtypeStruct((S, D_MLSTM), F32)] * 2,
        scratch_shapes=[pltpu.VMEM((nstate, MLSTM_HD, MLSTM_HD), F32),
                        pltpu.VMEM((nstate, 1, MLSTM_HD), F32),
                        pltpu.VMEM((nstate, 1, LANES), F32)],
        compiler_params=_cparams(("arbitrary",)),
        name="mlstm",
    )(u, u, u, u, u, u, u, u, bias_misc)


def _mlstm_out_kernel(hf_ref, hb_ref, o_ref, z_ref, g_ref, out_ref):
    for hd in range(MLSTM_HEADS):
        sl = slice(hd * MLSTM_HD, (hd + 1) * MLSTM_HD)
        h = hf_ref[:, sl] + hb_ref[:, sl]
        ms = jnp.mean(h * h, axis=-1, keepdims=True)
        hn = h * lax.rsqrt(ms + NORM_EPS) * g_ref[:, sl]
        out_ref[:, sl] = (_sigmoid(o_ref[:, sl]) * hn * _silu(z_ref[:, sl])).astype(out_ref.dtype)


def _mlstm_out(hf, hb, u, norm_g, tm):
    S = hf.shape[0]
    return pl.pallas_call(
        _mlstm_out_kernel,
        grid=(S // tm,),
        in_specs=[
            pl.BlockSpec((tm, D_MLSTM), lambda i: (i, 0)),
            pl.BlockSpec((tm, D_MLSTM), lambda i: (i, 0)),
            pl.BlockSpec((tm, D_MLSTM), lambda i: (i, OFF_MO // D_MLSTM)),
            pl.BlockSpec((tm, D_MLSTM), lambda i: (i, OFF_MZ // D_MLSTM)),
            pl.BlockSpec((1, D_MLSTM), lambda i: (0, 0)),
        ],
        out_specs=pl.BlockSpec((tm, D_MLSTM), lambda i: (i, 0)),
        out_shape=jax.ShapeDtypeStruct((S, D_MLSTM), BF16),
        compiler_params=_cparams(("parallel",)),
        name="mlstm_out",
    )(hf, hb, u, u, norm_g)


def _rope128(x, g, cos_ref, sin_ref):
    lane = lax.broadcasted_iota(jnp.int32, x.shape, 1)
    x = jnp.where(lane < MLA_ROPE, x, 0.0)
    ms = jnp.sum(x * x, axis=-1, keepdims=True) * (1.0 / MLA_ROPE)
    xn = x * lax.rsqrt(ms + NORM_EPS) * g
    half = MLA_ROPE // 2
    partner = jnp.where(lane < half, pltpu.roll(xn, LANES - half, axis=1), pltpu.roll(xn, half, axis=1))
    return xn * cos_ref[...] + partner * sin_ref[...]


def _q_up_kernel(x_ref, g_ref, w_ref, qn_ref, qr_ref, cos_ref, sin_ref, o_ref, h_ref, *, heads_per_tile):
    @pl.when(pl.program_id(1) == 0)
    def _():
        x = x_ref[...]
        ms = jnp.mean(x * x, axis=-1, keepdims=True)
        h_ref[...] = (x * lax.rsqrt(ms + NORM_EPS) * g_ref[...]).astype(BF16)

    acc = _dot(h_ref[...], w_ref[...])
    scale = MLA_QK ** -0.5 * LOG2_E
    for hh in range(heads_per_tile):
        nope = acc[:, hh * QK_PAD: hh * QK_PAD + MLA_NOPE]
        ms = jnp.mean(nope * nope, axis=-1, keepdims=True)
        nope = nope * lax.rsqrt(ms + NORM_EPS) * qn_ref[...]
        pe = _rope128(acc[:, hh * QK_PAD + MLA_NOPE: (hh + 1) * QK_PAD], qr_ref[...], cos_ref, sin_ref)
        o_ref[hh * QK_PAD: hh * QK_PAD + MLA_NOPE, :] = (nope * scale).T.astype(o_ref.dtype)
        o_ref[hh * QK_PAD + MLA_NOPE: (hh + 1) * QK_PAD, :] = (pe * scale).T.astype(o_ref.dtype)


def _q_up(u, g, w, qn_g, qr_g, cos_t, sin_t, tm, heads_per_tile):
    S = u.shape[0]
    tn = heads_per_tile * QK_PAD
    N = MLA_HEADS * QK_PAD
    return pl.pallas_call(
        functools.partial(_q_up_kernel, heads_per_tile=heads_per_tile),
        grid=(S // tm, N // tn),
        in_specs=[
            pl.BlockSpec((tm, Q_LORA), lambda i, j: (i, OFF_QLAT // Q_LORA)),
            pl.BlockSpec((1, Q_LORA), lambda i, j: (0, 0)),
            pl.BlockSpec((Q_LORA, tn), lambda i, j: (0, j)),
            pl.BlockSpec((1, LANES), lambda i, j: (0, 0)),
            pl.BlockSpec((1, LANES), lambda i, j: (0, 0)),
            pl.BlockSpec((tm, LANES), lambda i, j: (i, 0)),
            pl.BlockSpec((tm, LANES), lambda i, j: (i, 0)),
        ],
        out_specs=pl.BlockSpec((tn, tm), lambda i, j: (j, i)),
        out_shape=jax.ShapeDtypeStruct((N, S), BF16),
        scratch_shapes=[pltpu.VMEM((tm, Q_LORA), BF16)],
        compiler_params=_cparams(("parallel", "arbitrary")),
        name="q_up",
    )(u, g, w, qn_g, qr_g, cos_t, sin_t)


def _kv_up_kernel(x_ref, misc_ref, g_ref, w_ref, kn_ref, kr_ref, cos_ref, sin_ref, k_ref, vt_ref):
    x = x_ref[...]
    ms = jnp.mean(x * x, axis=-1, keepdims=True)
    h = (x * lax.rsqrt(ms + NORM_EPS) * g_ref[...]).astype(BF16)
    acc = _dot(h, w_ref[...])
    pe = _rope128(misc_ref[...], kr_ref[...], cos_ref, sin_ref).astype(k_ref.dtype)
    for hd in range(MLA_HEADS):
        nope = acc[:, hd * MLA_NOPE: (hd + 1) * MLA_NOPE]
        ms = jnp.mean(nope * nope, axis=-1, keepdims=True)
        k_ref[:, hd * QK_PAD: hd * QK_PAD + MLA_NOPE] = (nope * lax.rsqrt(ms + NORM_EPS) * kn_ref[...]).astype(k_ref.dtype)
        k_ref[:, hd * QK_PAD + MLA_NOPE: (hd + 1) * QK_PAD] = pe
        v = acc[:, MLA_HEADS * MLA_NOPE + hd * MLA_V: MLA_HEADS * MLA_NOPE + (hd + 1) * MLA_V]
        vt_ref[hd * V_AUG: hd * V_AUG + MLA_V, :] = v.T.astype(vt_ref.dtype)
        vt_ref[hd * V_AUG + MLA_V: (hd + 1) * V_AUG, :] = jnp.ones((V_ONES, vt_ref.shape[1]), vt_ref.dtype)


def _kv_up(u, g, w, kn_g, kr_g, cos_t, sin_t, tk):
    S = u.shape[0]
    return pl.pallas_call(
        _kv_up_kernel,
        grid=(S // tk,),
        in_specs=[
            pl.BlockSpec((tk, KV_LORA), lambda i: (i, OFF_KVLAT // KV_LORA)),
            pl.BlockSpec((tk, LANES), lambda i: (i, OFF_MISC // LANES)),
            pl.BlockSpec((1, KV_LORA), lambda i: (0, 0)),
            pl.BlockSpec((KV_LORA, 2 * D_MLA), lambda i: (0, 0)),
            pl.BlockSpec((1, LANES), lambda i: (0, 0)),
            pl.BlockSpec((1, LANES), lambda i: (0, 0)),
            pl.BlockSpec((tk, LANES), lambda i: (i, 0)),
            pl.BlockSpec((tk, LANES), lambda i: (i, 0)),
        ],
        out_specs=[pl.BlockSpec((tk, MLA_HEADS * QK_PAD), lambda i: (i, 0)),
                   pl.BlockSpec((None, MLA_HEADS * V_AUG, tk), lambda i: (i, 0, 0))],
        out_shape=[jax.ShapeDtypeStruct((S, MLA_HEADS * QK_PAD), BF16),
                   jax.ShapeDtypeStruct((S // tk, MLA_HEADS * V_AUG, tk), BF16)],
        compiler_params=_cparams(("parallel",)),
        name="kv_up",
    )(u, u, g, w, kn_g, kr_g, cos_t, sin_t)


def _attn_kernel(qt_ref, k_ref, vt_ref, z_ref, o_ref, s_ref, acc_ref):
    n_chunks, _, tk = vt_ref.shape
    tq = qt_ref.shape[1]

    def scores(j, slot):
        k = k_ref[pl.ds(pl.multiple_of(j * tk, tk), tk), :]
        st = _dot(k, qt_ref[...])
        s_ref[slot] = st
        return jnp.max(st, axis=0, keepdims=True)

    def accumulate(j, slot, m, cmax):
        m_new = jnp.maximum(m, cmax)
        alpha = jnp.exp2(m - m_new)
        p = jnp.exp2(s_ref[slot] - m_new).astype(BF16)
        acc_ref[...] = alpha * acc_ref[...] + _dot(vt_ref[j], p)
        return m_new

    n_slots = s_ref.shape[0]

    def group(j0, m, cmax, last):
        for u in range(n_slots):
            if not (last and u == n_slots - 1):
                cnext = scores(j0 + u + 1, (u + 1) % n_slots)
            m = accumulate(j0 + u, u, m, cmax)
            cmax = cnext
        return m, cmax

    acc_ref[...] = jnp.zeros_like(acc_ref)
    m = jnp.full((1, tq), NEG_BIG, F32)
    cmax = scores(0, 0)
    n_groups = n_chunks // n_slots
    m, cmax = lax.fori_loop(0, n_groups - 1, lambda g, c: group(g * n_slots, c[0], c[1], False), (m, cmax))
    group((n_groups - 1) * n_slots, m, cmax, True)
    o = (acc_ref[0:MLA_V, :] / acc_ref[MLA_V:MLA_V + 1, :]).T
    o_ref[...] = (o * _silu(z_ref[...])).astype(o_ref.dtype)


def _attn(qt, kf, vt, u, tq, n_slots):
    S = kf.shape[0]
    n_chunks, _, tk = vt.shape
    assert n_chunks % n_slots == 0
    return pl.pallas_call(
        _attn_kernel,
        grid=(MLA_HEADS, S // tq),
        in_specs=[
            pl.BlockSpec((QK_PAD, tq), lambda h, i: (h, i)),
            pl.BlockSpec((S, QK_PAD), lambda h, i: (0, h)),
            pl.BlockSpec((n_chunks, V_AUG, tk), lambda h, i: (0, h, 0)),
            pl.BlockSpec((tq, MLA_V), lambda h, i: (i, OFF_AZ // MLA_V + h)),
        ],
        out_specs=pl.BlockSpec((tq, MLA_V), lambda h, i: (i, h)),
        out_shape=jax.ShapeDtypeStruct((S, D_MLA), BF16),
        scratch_shapes=[pltpu.VMEM((n_slots, tk, tq), F32), pltpu.VMEM((V_AUG, tq), F32)],
        compiler_params=_cparams(("parallel", "arbitrary")),
        name="attn",
    )(qt, kf, vt, u)


def _out_proj_kernel(p_ref, m_ref, a_ref, w_ref, x_ref, o_ref):
    acc = _dot(p_ref[...], w_ref[0:D_POOL, :])
    acc += _dot(m_ref[...], w_ref[D_POOL:D_POOL + D_MLSTM, :])
    acc += _dot(a_ref[...], w_ref[D_POOL + D_MLSTM:, :])
    o_ref[...] = x_ref[...] + acc


def _out_proj(pool_o, mlstm_o, mla_o, w, x, tm, tn):
    S = x.shape[0]
    return pl.pallas_call(
        _out_proj_kernel,
        grid=(S // tm, D_MODEL // tn),
        in_specs=[
            pl.BlockSpec((tm, D_POOL), lambda i, j: (i, 0)),
            pl.BlockSpec((tm, D_MLSTM), lambda i, j: (i, 0)),
            pl.BlockSpec((tm, D_MLA), lambda i, j: (i, 0)),
            pl.BlockSpec((D_MODEL, tn), lambda i, j: (0, j)),
            pl.BlockSpec((tm, tn), lambda i, j: (i, j)),
        ],
        out_specs=pl.BlockSpec((tm, tn), lambda i, j: (i, j)),
        out_shape=jax.ShapeDtypeStruct((S, D_MODEL), F32),
        compiler_params=_cparams(("parallel", "arbitrary")),
        name="out_proj",
    )(pool_o, mlstm_o, mla_o, w, x)


def _pad_lanes(v, offset=0):
    return jnp.zeros((1, LANES), F32).at[0, offset:offset + v.shape[0]].set(v.astype(F32))


def _prep_layer(norm_g, w_in, gate_bias, pool_w, pool_scale, mlstm_norm_g,
                qlat_g, w_uq, kvlat_g, w_ukv, qn_g, qr_g, kn_g, kr_g, w_out):
    sizes = (D_POOL, D_POOL, D_MLSTM, D_MLSTM, D_MLSTM, D_MLSTM, D_MLSTM, MLSTM_GATES,
             Q_LORA, KV_LORA, MLA_ROPE, D_MLA)
    offs = np.concatenate([[0], np.cumsum(sizes)])
    (p_x, p_z, m_q, m_k, m_v, m_o, m_z, m_g, a_qlat, a_kvlat, a_krope, a_z) = [
        w_in[:, int(offs[i]):int(offs[i + 1])] for i in range(len(sizes))]
    zeros = lambda n: jnp.zeros((D_MODEL, n), w_in.dtype)
    w_in_p = jnp.concatenate(
        [a_qlat, a_kvlat, a_z, p_x, p_z, m_q, m_k, m_v, m_o, m_z, a_krope, m_g,
         zeros(LANES - MLA_ROPE - MLSTM_GATES), zeros(N_IN_PAD - OFF_MISC - LANES)], axis=1).astype(BF16)

    wq = w_uq.reshape(Q_LORA, MLA_HEADS, MLA_QK)
    wq = jnp.concatenate([wq, jnp.zeros((Q_LORA, MLA_HEADS, QK_PAD - MLA_QK), w_uq.dtype)], axis=-1)
    w_uq_p = wq.reshape(Q_LORA, MLA_HEADS * QK_PAD).astype(BF16)

    wkv = w_ukv.reshape(KV_LORA, MLA_HEADS, MLA_NOPE + MLA_V)
    w_ukv_p = jnp.concatenate([wkv[:, :, :MLA_NOPE].reshape(KV_LORA, -1),
                               wkv[:, :, MLA_NOPE:].reshape(KV_LORA, -1)], axis=1).astype(BF16)
    return dict(
        norm_g=norm_g.reshape(1, D_MODEL), w_in=w_in_p,
        bias_misc=_pad_lanes(gate_bias, MISC_GATE0),
        pool_w=pool_w.astype(BF16), pool_scale=pool_scale.reshape(1, D_POOL),
        mlstm_norm_g=mlstm_norm_g.reshape(1, D_MLSTM),
        qlat_g=qlat_g.reshape(1, Q_LORA), w_uq=w_uq_p,
        kvlat_g=kvlat_g.reshape(1, KV_LORA), w_ukv=w_ukv_p,
        qn_g=qn_g.reshape(1, MLA_NOPE), qr_g=_pad_lanes(qr_g),
        kn_g=kn_g.reshape(1, MLA_NOPE), kr_g=_pad_lanes(kr_g),
        w_out=w_out.astype(BF16),
    )


def _rope_tables(S):
    pos = jnp.arange(S, dtype=F32)
    inv_freq = ROPE_THETA ** (-(jnp.arange(0, MLA_ROPE, 2, dtype=F32) / MLA_ROPE))
    ang = pos[:, None] * inv_freq[None, :]
    cos, sin = jnp.cos(ang), jnp.sin(ang)
    pad = jnp.zeros((S, LANES - MLA_ROPE), F32)
    return jnp.concatenate([cos, cos, pad], axis=1), jnp.concatenate([-sin, sin, pad], axis=1)


def _tiles(S):
    t = lambda pref: min(pref, S)
    return dict(tm_in=t(512), tn_in=768, tm_pool=t(512), tm_mo=t(512), tm_q=t(512), q_heads=4,
                tk=t(512), tq=t(512), attn_slots=min(8, S // t(512)), tm_out=t(512), tn_out=1024)


def _layer(x, p, cos_t, sin_t):
    S = x.shape[0]
    t = _tiles(S)
    u = _in_proj(x, p["norm_g"], p["w_in"], t["tm_in"], t["tn_in"])
    pool_o = _pool(u, p["pool_w"], p["pool_scale"], t["tm_pool"])
    hf, hb = _mlstm(u, p["bias_misc"])
    mlstm_o = _mlstm_out(hf, hb, u, p["mlstm_norm_g"], t["tm_mo"])
    qt = _q_up(u, p["qlat_g"], p["w_uq"], p["qn_g"], p["qr_g"], cos_t, sin_t, t["tm_q"], t["q_heads"])
    kf, vt = _kv_up(u, p["kvlat_g"], p["w_ukv"], p["kn_g"], p["kr_g"], cos_t, sin_t, t["tk"])
    mla_o = _attn(qt, kf, vt, u, t["tq"], t["attn_slots"])
    return _out_proj(pool_o, mlstm_o, mla_o, p["w_out"], x, t["tm_out"], t["tn_out"])


def _trunk(x, layers):
    x = x[0]
    cos_t, sin_t = _rope_tables(x.shape[0])
    for p in layers:
        x = _layer(x, p, cos_t, sin_t)
    return x[None]


def kernel(x_prompt, x_sample, norm_g, w_in, gate_bias, pool_w, pool_scale, mlstm_norm_g,
           qlat_g, w_uq, kvlat_g, w_ukv, qn_g, qr_g, kn_g, kr_g, w_out):
    params = (norm_g, w_in, gate_bias, pool_w, pool_scale, mlstm_norm_g,
              qlat_g, w_uq, kvlat_g, w_ukv, qn_g, qr_g, kn_g, kr_g, w_out)
    layers = [_prep_layer(*[w[l] for w in params]) for l in range(norm_g.shape[0])]
    return (_trunk(x_prompt, layers), _trunk(x_sample, layers))
```

```python
import functools

import numpy as np
import jax
import jax.numpy as jnp
from jax import lax
from jax.experimental import pallas as pl
from jax.experimental.pallas import tpu as pltpu

D_MODEL = 4096
D_POOL = 1024
D_MLSTM = 1024
D_MLA = 2048
POOL_WINDOWS = (2, 4, 8, 16)
POOL_GW = 256
MLSTM_HEADS = 4
MLSTM_HD = 256
MLSTM_CHUNK = 128
MLSTM_GATES = 16
MLA_HEADS = 16
MLA_V = 128
MLA_NOPE = 128
MLA_ROPE = 64
MLA_QK = 192
Q_LORA = 1536
KV_LORA = 512
ROPE_THETA = 10000.0
NORM_EPS = 1e-6
N_IN = 11344

LANES = 128
QK_PAD = 256
V_ONES = 16
V_AUG = MLA_V + V_ONES
LOG2_E = 1.4426950408889634
VMEM_LIMIT = 56 * 1024 * 1024

OFF_QLAT = 0
OFF_KVLAT = 1536
OFF_AZ = 2048
OFF_PX = 4096
OFF_PZ = 5120
OFF_MQ = 6144
OFF_MK = 7168
OFF_MV = 8192
OFF_MO = 9216
OFF_MZ = 10240
OFF_MISC = 11264
N_IN_PAD = 11520
MISC_GATE0 = MLA_ROPE

NEG_BIG = -1e30

F32 = jnp.float32
BF16 = jnp.bfloat16


def _cparams(sem):
    return pltpu.CompilerParams(dimension_semantics=sem, vmem_limit_bytes=VMEM_LIMIT)


def _sigmoid(x):
    return 1.0 / (1.0 + jnp.exp(-x))


def _silu(x):
    return x * _sigmoid(x)


def _dot(a, b):
    return jnp.dot(a, b, preferred_element_type=F32)


def _dot_nt(a, b):
    return lax.dot_general(a, b, (((1,), (1,)), ((), ())), preferred_element_type=F32)


def _dot_tn(a, b):
    return lax.dot_general(a, b, (((0,), (0,)), ((), ())), preferred_element_type=F32)


def _norm_matmul_kernel(x_ref, g_ref, w_ref, o_ref, h_ref):
    @pl.when(pl.program_id(1) == 0)
    def _():
        x = x_ref[...]
        ms = jnp.mean(x * x, axis=-1, keepdims=True)
        h_ref[...] = (x * lax.rsqrt(ms + NORM_EPS) * g_ref[...]).astype(BF16)

    o_ref[...] = _dot(h_ref[...], w_ref[...]).astype(o_ref.dtype)


def _in_proj(x, g, w, tm, tn):
    S, K = x.shape
    N = w.shape[1]
    return pl.pallas_call(
        _norm_matmul_kernel,
        grid=(S // tm, N // tn),
        in_specs=[
            pl.BlockSpec((tm, K), lambda i, j: (i, 0)),
            pl.BlockSpec((1, K), lambda i, j: (0, 0)),
            pl.BlockSpec((K, tn), lambda i, j: (0, j)),
        ],
        out_specs=pl.BlockSpec((tm, tn), lambda i, j: (i, j)),
        out_shape=jax.ShapeDtypeStruct((S, N), F32),
        scratch_shapes=[pltpu.VMEM((tm, K), BF16)],
        compiler_params=_cparams(("parallel", "arbitrary")),
        name="in_proj",
    )(x, g, w)


POOL_HALO = 8


def _pool_kernel(prev_ref, cur_ref, next_ref, z_ref, w_ref, scale_ref, o_ref, *, seq_len):
    i = pl.program_id(0)
    tm = cur_ref.shape[0]
    rows = tm + 2 * POOL_HALO
    prev = jnp.where(i > 0, prev_ref[...], 0.0)
    nxt = jnp.where(i < pl.num_programs(0) - 1, next_ref[...], 0.0)
    cur = cur_ref[...]
    xall = jnp.concatenate([prev, cur, nxt], axis=0)
    pos = i * tm + lax.broadcasted_iota(jnp.int32, (tm, 1), 0)
    for g, win in enumerate(POOL_WINDOWS):
        half = win // 2
        sl = slice(g * POOL_GW, (g + 1) * POOL_GW)
        acc = xall[:, sl]
        span = 1
        while span < win:
            acc = acc + pltpu.roll(acc, span, axis=0)
            span *= 2
        shift = half - 1
        if shift:
            acc = pltpu.roll(acc, rows - shift, axis=0)
        wsum = acc[POOL_HALO:POOL_HALO + tm]
        cnt = jnp.minimum(pos + half, seq_len) - jnp.maximum(pos - half, 0)
        diff = wsum / cnt.astype(F32) - cur[:, sl]
        y = _dot(diff.astype(BF16), w_ref[g])
        o_ref[:, sl] = (y * scale_ref[:, sl] * _silu(z_ref[:, sl])).astype(o_ref.dtype)


def _pool(u, pool_w, pool_scale, tm):
    S = u.shape[0]
    hb = tm // POOL_HALO
    n_halo = S // POOL_HALO
    cpx = OFF_PX // D_POOL
    return pl.pallas_call(
        functools.partial(_pool_kernel, seq_len=S),
        grid=(S // tm,),
        in_specs=[
            pl.BlockSpec((POOL_HALO, D_POOL), lambda i: (jnp.maximum(i * hb - 1, 0), cpx)),
            pl.BlockSpec((tm, D_POOL), lambda i: (i, cpx)),
            pl.BlockSpec((POOL_HALO, D_POOL), lambda i: (jnp.minimum((i + 1) * hb, n_halo - 1), cpx)),
            pl.BlockSpec((tm, D_POOL), lambda i: (i, OFF_PZ // D_POOL)),
            pl.BlockSpec((len(POOL_WINDOWS), POOL_GW, POOL_GW), lambda i: (0, 0, 0)),
            pl.BlockSpec((1, D_POOL), lambda i: (0, 0)),
        ],
        out_specs=pl.BlockSpec((tm, D_POOL), lambda i: (i, 0)),
        out_shape=jax.ShapeDtypeStruct((S, D_POOL), BF16),
        compiler_params=_cparams(("parallel",)),
        name="pool",
    )(u, u, u, u, pool_w, pool_scale)


def _split3(x):
    hi = x.astype(BF16)
    r1 = x - hi.astype(F32)
    mid = r1.astype(BF16)
    lo = (r1 - mid.astype(F32)).astype(BF16)
    return hi, mid, lo


def _log_sigmoid(x):
    return jnp.minimum(x, 0.0) - jnp.log(1.0 + jnp.exp(-jnp.abs(x)))


def _lane_tile(x, reps):
    return jnp.concatenate([x] * reps, axis=1)


def _mlstm_chain(q, k, v, i_rep, b_rep, pm_rep, i_row, b_row, c_old, m, reverse):
    L = q.shape[0]
    row = lax.broadcasted_iota(jnp.int32, (L, L), 0)
    col = lax.broadcasted_iota(jnp.int32, (L, L), 1)
    keep = (col >= row) if reverse else (col <= row)
    b_last = b_rep[0:1] if reverse else b_rep[L - 1:L]

    m_t = b_rep + jnp.maximum(m, pm_rep)
    qb = q.astype(BF16)
    ks = k * (MLSTM_HD ** -0.5)
    v_aug = jnp.concatenate([v.astype(BF16), jnp.ones((L, LANES), BF16)], axis=1)
    dexp = jnp.exp(jnp.where(keep, (b_rep - m_t) - (b_row - i_row), NEG_BIG))
    w = dexp * _dot_nt(qb, ks.astype(BF16))
    ei = jnp.exp(b_rep + m - m_t)
    numden = (_lane_tile(ei, c_old.shape[1] // LANES) * _dot(qb, c_old.astype(BF16))
              + _dot(w.astype(BF16), v_aug))
    den = jnp.maximum(jnp.abs(numden[:, MLSTM_HD:]), jnp.exp(-m_t))
    h = numden[:, :MLSTM_HD] / _lane_tile(den, MLSTM_HD // LANES)

    a_rep = b_last - b_rep + i_rep
    m_new = jnp.maximum(b_last + m, jnp.max(a_rep, axis=0, keepdims=True))
    decay = jnp.exp(b_last + m - m_new)
    ek = _lane_tile(jnp.exp(a_rep - m_new), MLSTM_HD // LANES) * ks
    c_new = _lane_tile(decay, c_old.shape[1] // LANES) * c_old + _dot_tn(ek.astype(BF16), v_aug)
    return h, c_new, m_new


def _mlstm_kernel(qf_ref, kf_ref, vf_ref, gf_ref, qb_ref, kb_ref, vb_ref, gb_ref, bias_ref,
                  hf_ref, hb_ref, c_ref, m_ref):
    @pl.when(pl.program_id(0) == 0)
    def _():
        c_ref[...] = jnp.zeros_like(c_ref)
        m_ref[...] = jnp.zeros_like(m_ref)

    L = gf_ref.shape[0]
    row = lax.broadcasted_iota(jnp.int32, (L, L), 0)
    col = lax.broadcasted_iota(jnp.int32, (L, L), 1)
    dirs = ((qf_ref, kf_ref, vf_ref, gf_ref, hf_ref, False), (qb_ref, kb_ref, vb_ref, gb_ref, hb_ref, True))
    results = []
    for d, (q_ref, k_ref, v_ref, g_ref, h_ref, reverse) in enumerate(dirs):
        g = g_ref[...] + bias_ref[...]
        lane_i0 = MISC_GATE0 + (2 * d) * MLSTM_HEADS
        tri = jnp.where((col >= row) if reverse else (col <= row), 1.0, 0.0).astype(BF16)
        b_all = sum(_dot(tri, p) for p in _split3(_log_sigmoid(g)))
        pm_all = g - pltpu.roll(b_all, LANES - MLSTM_HEADS, axis=1)
        shift = 1
        while shift < L:
            if reverse:
                moved = jnp.where(row < L - shift, pltpu.roll(pm_all, L - shift, axis=0), NEG_BIG)
            else:
                moved = jnp.where(row >= shift, pltpu.roll(pm_all, shift, axis=0), NEG_BIG)
            pm_all = jnp.maximum(pm_all, moved)
            shift *= 2
        gt = g.T
        bt = b_all.T
        for hd in range(MLSTM_HEADS):
            sl = slice(hd * MLSTM_HD, (hd + 1) * MLSTM_HD)
            lane_i = lane_i0 + hd
            lane_f = lane_i0 + MLSTM_HEADS + hd
            idx = d * MLSTM_HEADS + hd
            lane_rep = lambda x, lane: jnp.broadcast_to(x[:, lane:lane + 1], (L, LANES))
            h, c_new, m_new = _mlstm_chain(
                q_ref[:, sl], k_ref[:, sl], v_ref[:, sl],
                lane_rep(g, lane_i), lane_rep(b_all, lane_f), lane_rep(pm_all, lane_i),
                gt[lane_i:lane_i + 1, :], bt[lane_f:lane_f + 1, :], c_ref[idx], m_ref[idx], reverse)
            results.append((h_ref, sl, idx, h, c_new, m_new))
    for h_ref, sl, idx, h, c_new, m_new in results:
        h_ref[:, sl] = h
        c_ref[idx] = c_new
        m_ref[idx] = m_new


def _mlstm(u, bias_misc):
    S = u.shape[0]
    L = MLSTM_CHUNK
    nc = S // L
    fwd = lambda c: c
    bwd = lambda c: nc - 1 - c

    def spec(off, width, cmap):
        return pl.BlockSpec((L, width), lambda c: (cmap(c), off // width))

    in_specs = []
    for cmap in (fwd, bwd):
        in_specs += [spec(OFF_MQ, D_MLSTM, cmap), spec(OFF_MK, D_MLSTM, cmap), spec(OFF_MV, D_MLSTM, cmap),
                     spec(OFF_MISC, LANES, cmap)]
    in_specs.append(pl.BlockSpec((1, LANES), lambda c: (0, 0)))
    nstate = 2 * MLSTM_HEADS
    return pl.pallas_call(
        _mlstm_kernel,
        grid=(nc,),
        in_specs=in_specs,
        out_specs=[pl.BlockSpec((L, D_MLSTM), lambda c: (c, 0)),
                   pl.BlockSpec((L, D_MLSTM), lambda c: (nc - 1 - c, 0))],
        out_shape=[jax.ShapeDtypeStruct((S, D_MLSTM), F32)] * 2,
        scratch_shapes=[pltpu.VMEM((nstate, MLSTM_HD, MLSTM_HD + LANES), F32),
                        pltpu.VMEM((nstate, 1, LANES), F32)],
        compiler_params=_cparams(("arbitrary",)),
        name="mlstm",
    )(u, u, u, u, u, u, u, u, bias_misc)


def _mlstm_out_kernel(hf_ref, hb_ref, o_ref, z_ref, g_ref, out_ref):
    for hd in range(MLSTM_HEADS):
        sl = slice(hd * MLSTM_HD, (hd + 1) * MLSTM_HD)
        h = hf_ref[:, sl] + hb_ref[:, sl]
        ms = jnp.mean(h * h, axis=-1, keepdims=True)
        hn = h * lax.rsqrt(ms + NORM_EPS) * g_ref[:, sl]
        out_ref[:, sl] = (_sigmoid(o_ref[:, sl]) * hn * _silu(z_ref[:, sl])).astype(out_ref.dtype)


def _mlstm_out(hf, hb, u, norm_g, tm):
    S = hf.shape[0]
    return pl.pallas_call(
        _mlstm_out_kernel,
        grid=(S // tm,),
        in_specs=[
            pl.BlockSpec((tm, D_MLSTM), lambda i: (i, 0)),
            pl.BlockSpec((tm, D_MLSTM), lambda i: (i, 0)),
            pl.BlockSpec((tm, D_MLSTM), lambda i: (i, OFF_MO // D_MLSTM)),
            pl.BlockSpec((tm, D_MLSTM), lambda i: (i, OFF_MZ // D_MLSTM)),
            pl.BlockSpec((1, D_MLSTM), lambda i: (0, 0)),
        ],
        out_specs=pl.BlockSpec((tm, D_MLSTM), lambda i: (i, 0)),
        out_shape=jax.ShapeDtypeStruct((S, D_MLSTM), BF16),
        compiler_params=_cparams(("parallel",)),
        name="mlstm_out",
    )(hf, hb, u, u, norm_g)


def _rope128(x, g, cos_ref, sin_ref):
    lane = lax.broadcasted_iota(jnp.int32, x.shape, 1)
    x = jnp.where(lane < MLA_ROPE, x, 0.0)
    ms = jnp.sum(x * x, axis=-1, keepdims=True) * (1.0 / MLA_ROPE)
    xn = x * lax.rsqrt(ms + NORM_EPS) * g
    half = MLA_ROPE // 2
    partner = jnp.where(lane < half, pltpu.roll(xn, LANES - half, axis=1), pltpu.roll(xn, half, axis=1))
    return xn * cos_ref[...] + partner * sin_ref[...]


def _q_up_kernel(x_ref, g_ref, w_ref, qn_ref, qr_ref, cos_ref, sin_ref, o_ref, h_ref, *, heads_per_tile):
    @pl.when(pl.program_id(1) == 0)
    def _():
        x = x_ref[...]
        ms = jnp.mean(x * x, axis=-1, keepdims=True)
        h_ref[...] = (x * lax.rsqrt(ms + NORM_EPS) * g_ref[...]).astype(BF16)

    scale = MLA_QK ** -0.5 * LOG2_E
    for hh in range(heads_per_tile):
        acc = _dot(h_ref[...], w_ref[:, hh * QK_PAD: (hh + 1) * QK_PAD])
        nope = acc[:, :MLA_NOPE]
        ms = jnp.mean(nope * nope, axis=-1, keepdims=True)
        nope = nope * lax.rsqrt(ms + NORM_EPS) * qn_ref[...]
        pe = _rope128(acc[:, MLA_NOPE:], qr_ref[...], cos_ref, sin_ref)
        o_ref[hh * QK_PAD: hh * QK_PAD + MLA_NOPE, :] = (nope * scale).astype(o_ref.dtype).T
        o_ref[hh * QK_PAD + MLA_NOPE: (hh + 1) * QK_PAD, :] = (pe * scale).astype(o_ref.dtype).T


def _q_up(u, g, w, qn_g, qr_g, cos_t, sin_t, tm, heads_per_tile):
    S = u.shape[0]
    tn = heads_per_tile * QK_PAD
    N = MLA_HEADS * QK_PAD
    return pl.pallas_call(
        functools.partial(_q_up_kernel, heads_per_tile=heads_per_tile),
        grid=(S // tm, N // tn),
        in_specs=[
            pl.BlockSpec((tm, Q_LORA), lambda i, j: (i, OFF_QLAT // Q_LORA)),
            pl.BlockSpec((1, Q_LORA), lambda i, j: (0, 0)),
            pl.BlockSpec((Q_LORA, tn), lambda i, j: (0, j)),
            pl.BlockSpec((1, LANES), lambda i, j: (0, 0)),
            pl.BlockSpec((1, LANES), lambda i, j: (0, 0)),
            pl.BlockSpec((tm, LANES), lambda i, j: (i, 0)),
            pl.BlockSpec((tm, LANES), lambda i, j: (i, 0)),
        ],
        out_specs=pl.BlockSpec((tn, tm), lambda i, j: (j, i)),
        out_shape=jax.ShapeDtypeStruct((N, S), BF16),
        scratch_shapes=[pltpu.VMEM((tm, Q_LORA), BF16)],
        compiler_params=_cparams(("parallel", "arbitrary")),
        name="q_up",
    )(u, g, w, qn_g, qr_g, cos_t, sin_t)


def _kv_up_kernel(x_ref, misc_ref, g_ref, w_ref, kn_ref, kr_ref, cos_ref, sin_ref, k_ref, vt_ref):
    x = x_ref[...]
    ms = jnp.mean(x * x, axis=-1, keepdims=True)
    h = (x * lax.rsqrt(ms + NORM_EPS) * g_ref[...]).astype(BF16)
    acc = _dot(h, w_ref[...])
    pe = _rope128(misc_ref[...], kr_ref[...], cos_ref, sin_ref).astype(k_ref.dtype)
    for hd in range(MLA_HEADS):
        nope = acc[:, hd * MLA_NOPE: (hd + 1) * MLA_NOPE]
        ms = jnp.mean(nope * nope, axis=-1, keepdims=True)
        k_ref[:, hd * QK_PAD: hd * QK_PAD + MLA_NOPE] = (nope * lax.rsqrt(ms + NORM_EPS) * kn_ref[...]).astype(k_ref.dtype)
        k_ref[:, hd * QK_PAD + MLA_NOPE: (hd + 1) * QK_PAD] = pe
        v = acc[:, MLA_HEADS * MLA_NOPE + hd * MLA_V: MLA_HEADS * MLA_NOPE + (hd + 1) * MLA_V]
        vt_ref[hd * V_AUG: hd * V_AUG + MLA_V, :] = v.astype(vt_ref.dtype).T
        vt_ref[hd * V_AUG + MLA_V: (hd + 1) * V_AUG, :] = jnp.ones((V_ONES, vt_ref.shape[1]), vt_ref.dtype)


def _kv_up(u, g, w, kn_g, kr_g, cos_t, sin_t, tk):
    S = u.shape[0]
    return pl.pallas_call(
        _kv_up_kernel,
        grid=(S // tk,),
        in_specs=[
            pl.BlockSpec((tk, KV_LORA), lambda i: (i, OFF_KVLAT // KV_LORA)),
            pl.BlockSpec((tk, LANES), lambda i: (i, OFF_MISC // LANES)),
            pl.BlockSpec((1, KV_LORA), lambda i: (0, 0)),
            pl.BlockSpec((KV_LORA, 2 * D_MLA), lambda i: (0, 0)),
            pl.BlockSpec((1, LANES), lambda i: (0, 0)),
            pl.BlockSpec((1, LANES), lambda i: (0, 0)),
            pl.BlockSpec((tk, LANES), lambda i: (i, 0)),
            pl.BlockSpec((tk, LANES), lambda i: (i, 0)),
        ],
        out_specs=[pl.BlockSpec((tk, MLA_HEADS * QK_PAD), lambda i: (i, 0)),
                   pl.BlockSpec((None, MLA_HEADS * V_AUG, tk), lambda i: (i, 0, 0))],
        out_shape=[jax.ShapeDtypeStruct((S, MLA_HEADS * QK_PAD), BF16),
                   jax.ShapeDtypeStruct((S // tk, MLA_HEADS * V_AUG, tk), BF16)],
        compiler_params=_cparams(("parallel",)),
        name="kv_up",
    )(u, u, g, w, kn_g, kr_g, cos_t, sin_t)


def _attn_kernel(qt_ref, qnext_ref, k_ref, vt_ref, z_ref, o_ref, s_ref, acc_ref, cmax_ref):
    n_chunks, _, tk = vt_ref.shape
    tq = qt_ref.shape[1]
    n_slots = s_ref.shape[0]

    def scores(q_ref, j, slot):
        k = k_ref[pl.ds(pl.multiple_of(j * tk, tk), tk), :]
        st = _dot(k, q_ref[...])
        s_ref[slot] = st
        return jnp.max(st, axis=0, keepdims=True)

    def accumulate(j, slot, m, cmax):
        m_new = jnp.maximum(m, cmax)
        alpha = jnp.exp2(m - m_new)
        p = jnp.exp2(s_ref[slot] - m_new).astype(BF16)
        acc_ref[...] = alpha * acc_ref[...] + _dot(vt_ref[j], p)
        return m_new

    def group(j0, m, cmax, last):
        for u in range(n_slots):
            if last and u == n_slots - 1:
                cnext = scores(qnext_ref, 0, 0)
            else:
                cnext = scores(qt_ref, j0 + u + 1, (u + 1) % n_slots)
            m = accumulate(j0 + u, u, m, cmax)
            cmax = cnext
        return m, cmax

    @pl.when(pl.program_id(1) == 0)
    def _():
        cmax_ref[...] = scores(qt_ref, 0, 0)

    acc_ref[...] = jnp.zeros_like(acc_ref)
    m = jnp.full((1, tq), NEG_BIG, F32)
    n_groups = n_chunks // n_slots
    m, cmax = lax.fori_loop(0, n_groups - 1, lambda g, c: group(g * n_slots, c[0], c[1], False),
                            (m, cmax_ref[...]))
    _, cmax_ref[...] = group((n_groups - 1) * n_slots, m, cmax, True)
    o = (acc_ref[0:MLA_V, :] / acc_ref[MLA_V:MLA_V + 1, :]).T
    o_ref[...] = (o * _silu(z_ref[...])).astype(o_ref.dtype)


def _attn(qt, kf, vt, u, tq, n_slots):
    S = kf.shape[0]
    n_chunks, _, tk = vt.shape
    n_q = S // tq
    assert n_chunks % n_slots == 0 and n_slots >= 2
    return pl.pallas_call(
        _attn_kernel,
        grid=(MLA_HEADS, n_q),
        in_specs=[
            pl.BlockSpec((QK_PAD, tq), lambda h, i: (h, i)),
            pl.BlockSpec((QK_PAD, tq), lambda h, i: (h, jnp.minimum(i + 1, n_q - 1))),
            pl.BlockSpec((S, QK_PAD), lambda h, i: (0, h)),
            pl.BlockSpec((n_chunks, V_AUG, tk), lambda h, i: (0, h, 0)),
            pl.BlockSpec((tq, MLA_V), lambda h, i: (i, OFF_AZ // MLA_V + h)),
        ],
        out_specs=pl.BlockSpec((tq, MLA_V), lambda h, i: (i, h)),
        out_shape=jax.ShapeDtypeStruct((S, D_MLA), BF16),
        scratch_shapes=[pltpu.VMEM((n_slots, tk, tq), F32), pltpu.VMEM((V_AUG, tq), F32),
                        pltpu.VMEM((1, tq), F32)],
        compiler_params=_cparams(("parallel", "arbitrary")),
        name="attn",
    )(qt, qt, kf, vt, u)


def _out_proj_kernel(p_ref, m_ref, a_ref, w_ref, x_ref, o_ref):
    acc = _dot(p_ref[...], w_ref[0:D_POOL, :])
    acc += _dot(m_ref[...], w_ref[D_POOL:D_POOL + D_MLSTM, :])
    acc += _dot(a_ref[...], w_ref[D_POOL + D_MLSTM:, :])
    o_ref[...] = x_ref[...] + acc


def _out_proj(pool_o, mlstm_o, mla_o, w, x, tm, tn):
    S = x.shape[0]
    return pl.pallas_call(
        _out_proj_kernel,
        grid=(S // tm, D_MODEL // tn),
        in_specs=[
            pl.BlockSpec((tm, D_POOL), lambda i, j: (i, 0)),
            pl.BlockSpec((tm, D_MLSTM), lambda i, j: (i, 0)),
            pl.BlockSpec((tm, D_MLA), lambda i, j: (i, 0)),
            pl.BlockSpec((D_MODEL, tn), lambda i, j: (0, j)),
            pl.BlockSpec((tm, tn), lambda i, j: (i, j)),
        ],
        out_specs=pl.BlockSpec((tm, tn), lambda i, j: (i, j)),
        out_shape=jax.ShapeDtypeStruct((S, D_MODEL), F32),
        compiler_params=_cparams(("parallel", "arbitrary")),
        name="out_proj",
    )(pool_o, mlstm_o, mla_o, w, x)


def _pad_lanes(v, offset=0):
    return jnp.zeros((1, LANES), F32).at[0, offset:offset + v.shape[0]].set(v.astype(F32))


def _prep_layer(norm_g, w_in, gate_bias, pool_w, pool_scale, mlstm_norm_g,
                qlat_g, w_uq, kvlat_g, w_ukv, qn_g, qr_g, kn_g, kr_g, w_out):
    sizes = (D_POOL, D_POOL, D_MLSTM, D_MLSTM, D_MLSTM, D_MLSTM, D_MLSTM, MLSTM_GATES,
             Q_LORA, KV_LORA, MLA_ROPE, D_MLA)
    offs = np.concatenate([[0], np.cumsum(sizes)])
    (p_x, p_z, m_q, m_k, m_v, m_o, m_z, m_g, a_qlat, a_kvlat, a_krope, a_z) = [
        w_in[:, int(offs[i]):int(offs[i + 1])] for i in range(len(sizes))]
    zeros = lambda n: jnp.zeros((D_MODEL, n), w_in.dtype)
    w_in_p = jnp.concatenate(
        [a_qlat, a_kvlat, a_z, p_x, p_z, m_q, m_k, m_v, m_o, m_z, a_krope, m_g,
         zeros(LANES - MLA_ROPE - MLSTM_GATES), zeros(N_IN_PAD - OFF_MISC - LANES)], axis=1).astype(BF16)

    wq = w_uq.reshape(Q_LORA, MLA_HEADS, MLA_QK)
    wq = jnp.concatenate([wq, jnp.zeros((Q_LORA, MLA_HEADS, QK_PAD - MLA_QK), w_uq.dtype)], axis=-1)
    w_uq_p = wq.reshape(Q_LORA, MLA_HEADS * QK_PAD).astype(BF16)

    wkv = w_ukv.reshape(KV_LORA, MLA_HEADS, MLA_NOPE + MLA_V)
    w_ukv_p = jnp.concatenate([wkv[:, :, :MLA_NOPE].reshape(KV_LORA, -1),
                               wkv[:, :, MLA_NOPE:].reshape(KV_LORA, -1)], axis=1).astype(BF16)
    return dict(
        norm_g=norm_g.reshape(1, D_MODEL), w_in=w_in_p,
        bias_misc=_pad_lanes(gate_bias, MISC_GATE0),
        pool_w=pool_w.astype(BF16), pool_scale=pool_scale.reshape(1, D_POOL),
        mlstm_norm_g=mlstm_norm_g.reshape(1, D_MLSTM),
        qlat_g=qlat_g.reshape(1, Q_LORA), w_uq=w_uq_p,
        kvlat_g=kvlat_g.reshape(1, KV_LORA), w_ukv=w_ukv_p,
        qn_g=qn_g.reshape(1, MLA_NOPE), qr_g=_pad_lanes(qr_g),
        kn_g=kn_g.reshape(1, MLA_NOPE), kr_g=_pad_lanes(kr_g),
        w_out=w_out.astype(BF16),
    )


def _rope_tables(S):
    pos = jnp.arange(S, dtype=F32)
    inv_freq = ROPE_THETA ** (-(jnp.arange(0, MLA_ROPE, 2, dtype=F32) / MLA_ROPE))
    ang = pos[:, None] * inv_freq[None, :]
    cos, sin = jnp.cos(ang), jnp.sin(ang)
    pad = jnp.zeros((S, LANES - MLA_ROPE), F32)
    return jnp.concatenate([cos, cos, pad], axis=1), jnp.concatenate([-sin, sin, pad], axis=1)


def _attn_slots(n_chunks):
    for slots in (8, 4, 2):
        if n_chunks % slots == 0 and n_chunks // slots >= 3:
            return slots
    return 2


def _tiles(S):
    t = lambda pref: min(pref, S)
    return dict(tm_in=t(512), tn_in=768, tm_pool=t(512), tm_mo=t(512), tm_q=t(512), q_heads=4,
                tk=t(512), tq=t(512), attn_slots=_attn_slots(S // t(512)), tm_out=t(512), tn_out=1024)


def _layer(x, p, cos_t, sin_t):
    S = x.shape[0]
    t = _tiles(S)
    u = _in_proj(x, p["norm_g"], p["w_in"], t["tm_in"], t["tn_in"])
    pool_o = _pool(u, p["pool_w"], p["pool_scale"], t["tm_pool"])
    hf, hb = _mlstm(u, p["bias_misc"])
    mlstm_o = _mlstm_out(hf, hb, u, p["mlstm_norm_g"], t["tm_mo"])
    qt = _q_up(u, p["qlat_g"], p["w_uq"], p["qn_g"], p["qr_g"], cos_t, sin_t, t["tm_q"], t["q_heads"])
    kf, vt = _kv_up(u, p["kvlat_g"], p["w_ukv"], p["kn_g"], p["kr_g"], cos_t, sin_t, t["tk"])
    mla_o = _attn(qt, kf, vt, u, t["tq"], t["attn_slots"])
    return _out_proj(pool_o, mlstm_o, mla_o, p["w_out"], x, t["tm_out"], t["tn_out"])


def _trunk(x, layers):
    x = x[0]
    cos_t, sin_t = _rope_tables(x.shape[0])
    for p in layers:
        x = _layer(x, p, cos_t, sin_t)
    return x[None]


def kernel(x_prompt, x_sample, norm_g, w_in, gate_bias, pool_w, pool_scale, mlstm_norm_g,
           qlat_g, w_uq, kvlat_g, w_ukv, qn_g, qr_g, kn_g, kr_g, w_out):
    params = (norm_g, w_in, gate_bias, pool_w, pool_scale, mlstm_norm_g,
              qlat_g, w_uq, kvlat_g, w_ukv, qn_g, qr_g, kn_g, kr_g, w_out)
    layers = [_prep_layer(*[w[l] for w in params]) for l in range(norm_g.shape[0])]
    return (_trunk(x_prompt, layers), _trunk(x_sample, layers))
```

```python
import functools

import numpy as np
import jax
import jax.numpy as jnp
from jax import lax
from jax.experimental import pallas as pl
from jax.experimental.pallas import tpu as pltpu

D_MODEL = 4096
D_POOL = 1024
D_MLSTM = 1024
D_MLA = 2048
POOL_WINDOWS = (2, 4, 8, 16)
POOL_GW = 256
MLSTM_HEADS = 4
MLSTM_HD = 256
MLSTM_CHUNK = 128
MLSTM_GATES = 16
MLA_HEADS = 16
MLA_V = 128
MLA_NOPE = 128
MLA_ROPE = 64
MLA_QK = 192
Q_LORA = 1536
KV_LORA = 512
ROPE_THETA = 10000.0
NORM_EPS = 1e-6
N_IN = 11344

LANES = 128
QK_PAD = 256
V_ONES = 16
V_AUG = MLA_V + V_ONES
LOG2_E = 1.4426950408889634
VMEM_LIMIT = 56 * 1024 * 1024

OFF_QLAT = 0
OFF_KVLAT = 1536
OFF_AZ = 2048
OFF_PX = 4096
OFF_PZ = 5120
OFF_MQ = 6144
OFF_MK = 7168
OFF_MV = 8192
OFF_MO = 9216
OFF_MZ = 10240
OFF_MISC = 11264
N_IN_PAD = 11520
MISC_GATE0 = MLA_ROPE

NEG_BIG = -1e30

F32 = jnp.float32
BF16 = jnp.bfloat16


def _cparams(sem):
    return pltpu.CompilerParams(dimension_semantics=sem, vmem_limit_bytes=VMEM_LIMIT)


def _sigmoid(x):
    return 1.0 / (1.0 + jnp.exp(-x))


def _silu(x):
    return x * _sigmoid(x)


def _dot(a, b):
    return jnp.dot(a, b, preferred_element_type=F32)


def _dot_nt(a, b):
    return lax.dot_general(a, b, (((1,), (1,)), ((), ())), preferred_element_type=F32)


def _dot_tn(a, b):
    return lax.dot_general(a, b, (((0,), (0,)), ((), ())), preferred_element_type=F32)


def _norm_matmul_kernel(x_ref, g_ref, w_ref, o_ref, h_ref):
    @pl.when(pl.program_id(1) == 0)
    def _():
        x = x_ref[...]
        ms = jnp.mean(x * x, axis=-1, keepdims=True)
        h_ref[...] = (x * lax.rsqrt(ms + NORM_EPS) * g_ref[...]).astype(BF16)

    o_ref[...] = _dot(h_ref[...], w_ref[...]).astype(o_ref.dtype)


def _in_proj(x, g, w, tm, tn):
    S, K = x.shape
    N = w.shape[1]
    return pl.pallas_call(
        _norm_matmul_kernel,
        grid=(S // tm, N // tn),
        in_specs=[
            pl.BlockSpec((tm, K), lambda i, j: (i, 0)),
            pl.BlockSpec((1, K), lambda i, j: (0, 0)),
            pl.BlockSpec((K, tn), lambda i, j: (0, j)),
        ],
        out_specs=pl.BlockSpec((tm, tn), lambda i, j: (i, j)),
        out_shape=jax.ShapeDtypeStruct((S, N), F32),
        scratch_shapes=[pltpu.VMEM((tm, K), BF16)],
        compiler_params=_cparams(("parallel", "arbitrary")),
        name="in_proj",
    )(x, g, w)


POOL_HALO = 8


def _pool_kernel(prev_ref, cur_ref, next_ref, z_ref, w_ref, scale_ref, o_ref, *, seq_len):
    i = pl.program_id(0)
    tm = cur_ref.shape[0]
    rows = tm + 2 * POOL_HALO
    prev = jnp.where(i > 0, prev_ref[...], 0.0)
    nxt = jnp.where(i < pl.num_programs(0) - 1, next_ref[...], 0.0)
    cur = cur_ref[...]
    xall = jnp.concatenate([prev, cur, nxt], axis=0)
    pos = i * tm + lax.broadcasted_iota(jnp.int32, (tm, 1), 0)
    for g, win in enumerate(POOL_WINDOWS):
        half = win // 2
        sl = slice(g * POOL_GW, (g + 1) * POOL_GW)
        acc = xall[:, sl]
        span = 1
        while span < win:
            acc = acc + pltpu.roll(acc, span, axis=0)
            span *= 2
        shift = half - 1
        if shift:
            acc = pltpu.roll(acc, rows - shift, axis=0)
        wsum = acc[POOL_HALO:POOL_HALO + tm]
        cnt = jnp.minimum(pos + half, seq_len) - jnp.maximum(pos - half, 0)
        diff = wsum / cnt.astype(F32) - cur[:, sl]
        y = _dot(diff.astype(BF16), w_ref[g])
        o_ref[:, sl] = (y * scale_ref[:, sl] * _silu(z_ref[:, sl])).astype(o_ref.dtype)


def _pool(u, pool_w, pool_scale, tm):
    S = u.shape[0]
    hb = tm // POOL_HALO
    n_halo = S // POOL_HALO
    cpx = OFF_PX // D_POOL
    return pl.pallas_call(
        functools.partial(_pool_kernel, seq_len=S),
        grid=(S // tm,),
        in_specs=[
            pl.BlockSpec((POOL_HALO, D_POOL), lambda i: (jnp.maximum(i * hb - 1, 0), cpx)),
            pl.BlockSpec((tm, D_POOL), lambda i: (i, cpx)),
            pl.BlockSpec((POOL_HALO, D_POOL), lambda i: (jnp.minimum((i + 1) * hb, n_halo - 1), cpx)),
            pl.BlockSpec((tm, D_POOL), lambda i: (i, OFF_PZ // D_POOL)),
            pl.BlockSpec((len(POOL_WINDOWS), POOL_GW, POOL_GW), lambda i: (0, 0, 0)),
            pl.BlockSpec((1, D_POOL), lambda i: (0, 0)),
        ],
        out_specs=pl.BlockSpec((tm, D_POOL), lambda i: (i, 0)),
        out_shape=jax.ShapeDtypeStruct((S, D_POOL), BF16),
        compiler_params=_cparams(("parallel",)),
        name="pool",
    )(u, u, u, u, pool_w, pool_scale)


def _split3(x):
    hi = x.astype(BF16)
    r1 = x - hi.astype(F32)
    mid = r1.astype(BF16)
    lo = (r1 - mid.astype(F32)).astype(BF16)
    return hi, mid, lo


def _log_sigmoid(x):
    return jnp.minimum(x, 0.0) - jnp.log(1.0 + jnp.exp(-jnp.abs(x)))


def _lane_tile(x, reps):
    return jnp.concatenate([x] * reps, axis=1)


def _mlstm_chain(q, k, v, i_rep, b_rep, pm_rep, i_row, b_row, c_old, m, reverse):
    L = q.shape[0]
    row = lax.broadcasted_iota(jnp.int32, (L, L), 0)
    col = lax.broadcasted_iota(jnp.int32, (L, L), 1)
    keep = (col >= row) if reverse else (col <= row)
    b_last = b_rep[0:1] if reverse else b_rep[L - 1:L]

    m_t = b_rep + jnp.maximum(m, pm_rep)
    qb = q.astype(BF16)
    ks = k * (MLSTM_HD ** -0.5)
    v_aug = jnp.concatenate([v.astype(BF16), jnp.ones((L, LANES), BF16)], axis=1)
    dexp = jnp.exp(jnp.where(keep, (b_rep - m_t) - (b_row - i_row), NEG_BIG))
    w = dexp * _dot_nt(qb, ks.astype(BF16))
    ei = jnp.exp(b_rep + m - m_t)
    numden = (_lane_tile(ei, c_old.shape[1] // LANES) * _dot(qb, c_old.astype(BF16))
              + _dot(w.astype(BF16), v_aug))
    den = jnp.maximum(jnp.abs(numden[:, MLSTM_HD:]), jnp.exp(-m_t))
    h = numden[:, :MLSTM_HD] / _lane_tile(den, MLSTM_HD // LANES)

    a_rep = b_last - b_rep + i_rep
    m_new = jnp.maximum(b_last + m, jnp.max(a_rep, axis=0, keepdims=True))
    decay = jnp.exp(b_last + m - m_new)
    ek = _lane_tile(jnp.exp(a_rep - m_new), MLSTM_HD // LANES) * ks
    c_new = _lane_tile(decay, c_old.shape[1] // LANES) * c_old + _dot_tn(ek.astype(BF16), v_aug)
    return h, c_new, m_new


def _mlstm_kernel(qf_ref, kf_ref, vf_ref, gf_ref, qb_ref, kb_ref, vb_ref, gb_ref, bias_ref,
                  hf_ref, hb_ref, c_ref, m_ref):
    @pl.when(pl.program_id(0) == 0)
    def _():
        c_ref[...] = jnp.zeros_like(c_ref)
        m_ref[...] = jnp.zeros_like(m_ref)

    L = gf_ref.shape[0]
    row = lax.broadcasted_iota(jnp.int32, (L, L), 0)
    col = lax.broadcasted_iota(jnp.int32, (L, L), 1)
    dirs = ((qf_ref, kf_ref, vf_ref, gf_ref, hf_ref, False), (qb_ref, kb_ref, vb_ref, gb_ref, hb_ref, True))
    results = []
    for d, (q_ref, k_ref, v_ref, g_ref, h_ref, reverse) in enumerate(dirs):
        g = g_ref[...] + bias_ref[...]
        lane_i0 = MISC_GATE0 + (2 * d) * MLSTM_HEADS
        tri = jnp.where((col >= row) if reverse else (col <= row), 1.0, 0.0).astype(BF16)
        b_all = sum(_dot(tri, p) for p in _split3(_log_sigmoid(g)))
        pm_all = g - pltpu.roll(b_all, LANES - MLSTM_HEADS, axis=1)
        shift = 1
        while shift < L:
            if reverse:
                moved = jnp.where(row < L - shift, pltpu.roll(pm_all, L - shift, axis=0), NEG_BIG)
            else:
                moved = jnp.where(row >= shift, pltpu.roll(pm_all, shift, axis=0), NEG_BIG)
            pm_all = jnp.maximum(pm_all, moved)
            shift *= 2
        gt = g.T
        bt = b_all.T
        for hd in range(MLSTM_HEADS):
            sl = slice(hd * MLSTM_HD, (hd + 1) * MLSTM_HD)
            lane_i = lane_i0 + hd
            lane_f = lane_i0 + MLSTM_HEADS + hd
            idx = d * MLSTM_HEADS + hd
            lane_rep = lambda x, lane: jnp.broadcast_to(x[:, lane:lane + 1], (L, LANES))
            h, c_new, m_new = _mlstm_chain(
                q_ref[:, sl], k_ref[:, sl], v_ref[:, sl],
                lane_rep(g, lane_i), lane_rep(b_all, lane_f), lane_rep(pm_all, lane_i),
                gt[lane_i:lane_i + 1, :], bt[lane_f:lane_f + 1, :], c_ref[idx], m_ref[idx], reverse)
            results.append((h_ref, sl, idx, h, c_new, m_new))
    for h_ref, sl, idx, h, c_new, m_new in results:
        h_ref[:, sl] = h
        c_ref[idx] = c_new
        m_ref[idx] = m_new


def _mlstm(u, bias_misc):
    S = u.shape[0]
    L = MLSTM_CHUNK
    nc = S // L
    fwd = lambda c: c
    bwd = lambda c: nc - 1 - c

    def spec(off, width, cmap):
        return pl.BlockSpec((L, width), lambda c: (cmap(c), off // width))

    in_specs = []
    for cmap in (fwd, bwd):
        in_specs += [spec(OFF_MQ, D_MLSTM, cmap), spec(OFF_MK, D_MLSTM, cmap), spec(OFF_MV, D_MLSTM, cmap),
                     spec(OFF_MISC, LANES, cmap)]
    in_specs.append(pl.BlockSpec((1, LANES), lambda c: (0, 0)))
    nstate = 2 * MLSTM_HEADS
    return pl.pallas_call(
        _mlstm_kernel,
        grid=(nc,),
        in_specs=in_specs,
        out_specs=[pl.BlockSpec((L, D_MLSTM), lambda c: (c, 0)),
                   pl.BlockSpec((L, D_MLSTM), lambda c: (nc - 1 - c, 0))],
        out_shape=[jax.ShapeDtypeStruct((S, D_MLSTM), F32)] * 2,
        scratch_shapes=[pltpu.VMEM((nstate, MLSTM_HD, MLSTM_HD + LANES), F32),
                        pltpu.VMEM((nstate, 1, LANES), F32)],
        compiler_params=_cparams(("arbitrary",)),
        name="mlstm",
    )(u, u, u, u, u, u, u, u, bias_misc)


def _mlstm_out_kernel(hf_ref, hb_ref, o_ref, z_ref, g_ref, out_ref):
    for hd in range(MLSTM_HEADS):
        sl = slice(hd * MLSTM_HD, (hd + 1) * MLSTM_HD)
        h = hf_ref[:, sl] + hb_ref[:, sl]
        ms = jnp.mean(h * h, axis=-1, keepdims=True)
        hn = h * lax.rsqrt(ms + NORM_EPS) * g_ref[:, sl]
        out_ref[:, sl] = (_sigmoid(o_ref[:, sl]) * hn * _silu(z_ref[:, sl])).astype(out_ref.dtype)


def _mlstm_out(hf, hb, u, norm_g, tm):
    S = hf.shape[0]
    return pl.pallas_call(
        _mlstm_out_kernel,
        grid=(S // tm,),
        in_specs=[
            pl.BlockSpec((tm, D_MLSTM), lambda i: (i, 0)),
            pl.BlockSpec((tm, D_MLSTM), lambda i: (i, 0)),
            pl.BlockSpec((tm, D_MLSTM), lambda i: (i, OFF_MO // D_MLSTM)),
            pl.BlockSpec((tm, D_MLSTM), lambda i: (i, OFF_MZ // D_MLSTM)),
            pl.BlockSpec((1, D_MLSTM), lambda i: (0, 0)),
        ],
        out_specs=pl.BlockSpec((tm, D_MLSTM), lambda i: (i, 0)),
        out_shape=jax.ShapeDtypeStruct((S, D_MLSTM), BF16),
        compiler_params=_cparams(("parallel",)),
        name="mlstm_out",
    )(hf, hb, u, u, norm_g)


def _rope128(x, g, cos_ref, sin_ref):
    lane = lax.broadcasted_iota(jnp.int32, x.shape, 1)
    x = jnp.where(lane < MLA_ROPE, x, 0.0)
    ms = jnp.sum(x * x, axis=-1, keepdims=True) * (1.0 / MLA_ROPE)
    xn = x * lax.rsqrt(ms + NORM_EPS) * g
    half = MLA_ROPE // 2
    partner = jnp.where(lane < half, pltpu.roll(xn, LANES - half, axis=1), pltpu.roll(xn, half, axis=1))
    return xn * cos_ref[...] + partner * sin_ref[...]


def _q_up_kernel(x_ref, g_ref, w_ref, qn_ref, qr_ref, cos_ref, sin_ref, o_ref, h_ref, acc_ref,
                 *, heads_per_tile, n_col, n_tiles):
    s = pl.program_id(0)

    @pl.when(jnp.logical_and(s % n_col == 0, s < n_tiles))
    def _():
        x = x_ref[...]
        ms = jnp.mean(x * x, axis=-1, keepdims=True)
        h_ref[...] = (x * lax.rsqrt(ms + NORM_EPS) * g_ref[...]).astype(BF16)

    @pl.when(s == 0)
    def _():
        acc_ref[1] = jnp.zeros(acc_ref.shape[1:], acc_ref.dtype)

    scale = MLA_QK ** -0.5 * LOG2_E

    def step(fill, drain):
        for hh in range(heads_per_tile):
            cols = slice(hh * QK_PAD, (hh + 1) * QK_PAD)
            acc_ref[fill, :, cols] = _dot(h_ref[...], w_ref[:, cols])
        for hh in range(heads_per_tile):
            nope = acc_ref[drain, :, hh * QK_PAD: hh * QK_PAD + MLA_NOPE]
            ms = jnp.mean(nope * nope, axis=-1, keepdims=True)
            nope = nope * lax.rsqrt(ms + NORM_EPS) * qn_ref[...]
            pe = _rope128(acc_ref[drain, :, hh * QK_PAD + MLA_NOPE: (hh + 1) * QK_PAD], qr_ref[...], cos_ref, sin_ref)
            o_ref[hh * QK_PAD: hh * QK_PAD + MLA_NOPE, :] = (nope * scale).astype(o_ref.dtype).T
            o_ref[hh * QK_PAD + MLA_NOPE: (hh + 1) * QK_PAD, :] = (pe * scale).astype(o_ref.dtype).T

    @pl.when(s % 2 == 0)
    def _():
        step(0, 1)

    @pl.when(s % 2 == 1)
    def _():
        step(1, 0)


def _q_up(u, g, w, qn_g, qr_g, cos_t, sin_t, tm, heads_per_tile):
    S = u.shape[0]
    tn = heads_per_tile * QK_PAD
    N = MLA_HEADS * QK_PAD
    n_col = N // tn
    n_tiles = (S // tm) * n_col
    fill_row = lambda s: jnp.minimum(s, n_tiles - 1) // n_col
    fill_col = lambda s: jnp.minimum(s, n_tiles - 1) % n_col
    drain_row = lambda s: jnp.maximum(s - 1, 0) // n_col
    drain_col = lambda s: jnp.maximum(s - 1, 0) % n_col
    return pl.pallas_call(
        functools.partial(_q_up_kernel, heads_per_tile=heads_per_tile, n_col=n_col, n_tiles=n_tiles),
        grid=(n_tiles + 1,),
        in_specs=[
            pl.BlockSpec((tm, Q_LORA), lambda s: (fill_row(s), OFF_QLAT // Q_LORA)),
            pl.BlockSpec((1, Q_LORA), lambda s: (0, 0)),
            pl.BlockSpec((Q_LORA, tn), lambda s: (0, fill_col(s))),
            pl.BlockSpec((1, LANES), lambda s: (0, 0)),
            pl.BlockSpec((1, LANES), lambda s: (0, 0)),
            pl.BlockSpec((tm, LANES), lambda s: (drain_row(s), 0)),
            pl.BlockSpec((tm, LANES), lambda s: (drain_row(s), 0)),
        ],
        out_specs=pl.BlockSpec((tn, tm), lambda s: (drain_col(s), drain_row(s))),
        out_shape=jax.ShapeDtypeStruct((N, S), BF16),
        scratch_shapes=[pltpu.VMEM((tm, Q_LORA), BF16), pltpu.VMEM((2, tm, tn), F32)],
        compiler_params=_cparams(("arbitrary",)),
        name="q_up",
    )(u, g, w, qn_g, qr_g, cos_t, sin_t)


def _kv_up_kernel(x_ref, misc_ref, g_ref, w_ref, kn_ref, kr_ref, cos_ref, sin_ref, k_ref, vt_ref):
    x = x_ref[...]
    ms = jnp.mean(x * x, axis=-1, keepdims=True)
    h = (x * lax.rsqrt(ms + NORM_EPS) * g_ref[...]).astype(BF16)
    acc = _dot(h, w_ref[...])
    pe = _rope128(misc_ref[...], kr_ref[...], cos_ref, sin_ref).astype(k_ref.dtype)
    for hd in range(MLA_HEADS):
        nope = acc[:, hd * MLA_NOPE: (hd + 1) * MLA_NOPE]
        ms = jnp.mean(nope * nope, axis=-1, keepdims=True)
        k_ref[:, hd * QK_PAD: hd * QK_PAD + MLA_NOPE] = (nope * lax.rsqrt(ms + NORM_EPS) * kn_ref[...]).astype(k_ref.dtype)
        k_ref[:, hd * QK_PAD + MLA_NOPE: (hd + 1) * QK_PAD] = pe
        v = acc[:, MLA_HEADS * MLA_NOPE + hd * MLA_V: MLA_HEADS * MLA_NOPE + (hd + 1) * MLA_V]
        vt_ref[hd * V_AUG: hd * V_AUG + MLA_V, :] = v.astype(vt_ref.dtype).T
        vt_ref[hd * V_AUG + MLA_V: (hd + 1) * V_AUG, :] = jnp.ones((V_ONES, vt_ref.shape[1]), vt_ref.dtype)


def _kv_up(u, g, w, kn_g, kr_g, cos_t, sin_t, tk):
    S = u.shape[0]
    return pl.pallas_call(
        _kv_up_kernel,
        grid=(S // tk,),
        in_specs=[
            pl.BlockSpec((tk, KV_LORA), lambda i: (i, OFF_KVLAT // KV_LORA)),
            pl.BlockSpec((tk, LANES), lambda i: (i, OFF_MISC // LANES)),
            pl.BlockSpec((1, KV_LORA), lambda i: (0, 0)),
            pl.BlockSpec((KV_LORA, 2 * D_MLA), lambda i: (0, 0)),
            pl.BlockSpec((1, LANES), lambda i: (0, 0)),
            pl.BlockSpec((1, LANES), lambda i: (0, 0)),
            pl.BlockSpec((tk, LANES), lambda i: (i, 0)),
            pl.BlockSpec((tk, LANES), lambda i: (i, 0)),
        ],
        out_specs=[pl.BlockSpec((tk, MLA_HEADS * QK_PAD), lambda i: (i, 0)),
                   pl.BlockSpec((None, MLA_HEADS * V_AUG, tk), lambda i: (i, 0, 0))],
        out_shape=[jax.ShapeDtypeStruct((S, MLA_HEADS * QK_PAD), BF16),
                   jax.ShapeDtypeStruct((S // tk, MLA_HEADS * V_AUG, tk), BF16)],
        compiler_params=_cparams(("parallel",)),
        name="kv_up",
    )(u, u, g, w, kn_g, kr_g, cos_t, sin_t)


def _attn_kernel(qt_ref, qnext_ref, k_ref, vt_ref, z_ref, o_ref, s_ref, acc_ref, cmax_ref, *, group_size):
    n_chunks, _, tk = vt_ref.shape
    tq = qt_ref.shape[1]
    n_slots = s_ref.shape[0]

    def scores(q_ref, j, slot):
        k = k_ref[pl.ds(pl.multiple_of(j * tk, tk), tk), :]
        st = _dot(k, q_ref[...])
        s_ref[slot] = st
        return jnp.max(st, axis=0, keepdims=True)

    def accumulate(j, slot, m, cmax):
        m_new = jnp.maximum(m, cmax)
        alpha = jnp.exp2(m - m_new)
        p = jnp.exp2(s_ref[slot] - m_new).astype(BF16)
        acc_ref[...] = alpha * acc_ref[...] + _dot(vt_ref[j], p)
        return m_new

    def group(j0, m, cmax, last):
        for u in range(group_size):
            if last and u == group_size - 1:
                cnext = scores(qnext_ref, 0, 0)
            else:
                cnext = scores(qt_ref, j0 + u + 1, (u + 1) % n_slots)
            m = accumulate(j0 + u, u % n_slots, m, cmax)
            cmax = cnext
        return m, cmax

    @pl.when(pl.program_id(1) == 0)
    def _():
        cmax_ref[...] = scores(qt_ref, 0, 0)

    acc_ref[...] = jnp.zeros_like(acc_ref)
    m = jnp.full((1, tq), NEG_BIG, F32)
    n_groups = n_chunks // group_size
    m, cmax = lax.fori_loop(0, n_groups - 1, lambda g, c: group(g * group_size, c[0], c[1], False),
                            (m, cmax_ref[...]))
    _, cmax_ref[...] = group((n_groups - 1) * group_size, m, cmax, True)
    o = (acc_ref[0:MLA_V, :] / acc_ref[MLA_V:MLA_V + 1, :]).T
    o_ref[...] = (o * _silu(z_ref[...])).astype(o_ref.dtype)


def _attn(qt, kf, vt, u, tq, group_size, n_slots):
    S = kf.shape[0]
    n_chunks, _, tk = vt.shape
    n_q = S // tq
    assert n_chunks % group_size == 0 and group_size % n_slots == 0 and n_slots >= 2
    return pl.pallas_call(
        functools.partial(_attn_kernel, group_size=group_size),
        grid=(MLA_HEADS, n_q),
        in_specs=[
            pl.BlockSpec((QK_PAD, tq), lambda h, i: (h, i)),
            pl.BlockSpec((QK_PAD, tq), lambda h, i: (h, jnp.minimum(i + 1, n_q - 1))),
            pl.BlockSpec((S, QK_PAD), lambda h, i: (0, h)),
            pl.BlockSpec((n_chunks, V_AUG, tk), lambda h, i: (0, h, 0)),
            pl.BlockSpec((tq, MLA_V), lambda h, i: (i, OFF_AZ // MLA_V + h)),
        ],
        out_specs=pl.BlockSpec((tq, MLA_V), lambda h, i: (i, h)),
        out_shape=jax.ShapeDtypeStruct((S, D_MLA), BF16),
        scratch_shapes=[pltpu.VMEM((n_slots, tk, tq), F32), pltpu.VMEM((V_AUG, tq), F32),
                        pltpu.VMEM((1, tq), F32)],
        compiler_params=_cparams(("parallel", "arbitrary")),
        name="attn",
    )(qt, qt, kf, vt, u)


def _out_proj_kernel(p_ref, m_ref, a_ref, w_ref, x_ref, o_ref):
    acc = _dot(p_ref[...], w_ref[0:D_POOL, :])
    acc += _dot(m_ref[...], w_ref[D_POOL:D_POOL + D_MLSTM, :])
    acc += _dot(a_ref[...], w_ref[D_POOL + D_MLSTM:, :])
    o_ref[...] = x_ref[...] + acc


def _out_proj(pool_o, mlstm_o, mla_o, w, x, tm, tn):
    S = x.shape[0]
    return pl.pallas_call(
        _out_proj_kernel,
        grid=(S // tm, D_MODEL // tn),
        in_specs=[
            pl.BlockSpec((tm, D_POOL), lambda i, j: (i, 0)),
            pl.BlockSpec((tm, D_MLSTM), lambda i, j: (i, 0)),
            pl.BlockSpec((tm, D_MLA), lambda i, j: (i, 0)),
            pl.BlockSpec((D_MODEL, tn), lambda i, j: (0, j)),
            pl.BlockSpec((tm, tn), lambda i, j: (i, j)),
        ],
        out_specs=pl.BlockSpec((tm, tn), lambda i, j: (i, j)),
        out_shape=jax.ShapeDtypeStruct((S, D_MODEL), F32),
        compiler_params=_cparams(("parallel", "arbitrary")),
        name="out_proj",
    )(pool_o, mlstm_o, mla_o, w, x)


def _pad_lanes(v, offset=0):
    return jnp.zeros((1, LANES), F32).at[0, offset:offset + v.shape[0]].set(v.astype(F32))


def _prep_layer(norm_g, w_in, gate_bias, pool_w, pool_scale, mlstm_norm_g,
                qlat_g, w_uq, kvlat_g, w_ukv, qn_g, qr_g, kn_g, kr_g, w_out):
    sizes = (D_POOL, D_POOL, D_MLSTM, D_MLSTM, D_MLSTM, D_MLSTM, D_MLSTM, MLSTM_GATES,
             Q_LORA, KV_LORA, MLA_ROPE, D_MLA)
    offs = np.concatenate([[0], np.cumsum(sizes)])
    (p_x, p_z, m_q, m_k, m_v, m_o, m_z, m_g, a_qlat, a_kvlat, a_krope, a_z) = [
        w_in[:, int(offs[i]):int(offs[i + 1])] for i in range(len(sizes))]
    zeros = lambda n: jnp.zeros((D_MODEL, n), w_in.dtype)
    w_in_p = jnp.concatenate(
        [a_qlat, a_kvlat, a_z, p_x, p_z, m_q, m_k, m_v, m_o, m_z, a_krope, m_g,
         zeros(LANES - MLA_ROPE - MLSTM_GATES), zeros(N_IN_PAD - OFF_MISC - LANES)], axis=1).astype(BF16)

    wq = w_uq.reshape(Q_LORA, MLA_HEADS, MLA_QK)
    wq = jnp.concatenate([wq, jnp.zeros((Q_LORA, MLA_HEADS, QK_PAD - MLA_QK), w_uq.dtype)], axis=-1)
    w_uq_p = wq.reshape(Q_LORA, MLA_HEADS * QK_PAD).astype(BF16)

    wkv = w_ukv.reshape(KV_LORA, MLA_HEADS, MLA_NOPE + MLA_V)
    w_ukv_p = jnp.concatenate([wkv[:, :, :MLA_NOPE].reshape(KV_LORA, -1),
                               wkv[:, :, MLA_NOPE:].reshape(KV_LORA, -1)], axis=1).astype(BF16)
    return dict(
        norm_g=norm_g.reshape(1, D_MODEL), w_in=w_in_p,
        bias_misc=_pad_lanes(gate_bias, MISC_GATE0),
        pool_w=pool_w.astype(BF16), pool_scale=pool_scale.reshape(1, D_POOL),
        mlstm_norm_g=mlstm_norm_g.reshape(1, D_MLSTM),
        qlat_g=qlat_g.reshape(1, Q_LORA), w_uq=w_uq_p,
        kvlat_g=kvlat_g.reshape(1, KV_LORA), w_ukv=w_ukv_p,
        qn_g=qn_g.reshape(1, MLA_NOPE), qr_g=_pad_lanes(qr_g),
        kn_g=kn_g.reshape(1, MLA_NOPE), kr_g=_pad_lanes(kr_g),
        w_out=w_out.astype(BF16),
    )


def _rope_tables(S):
    pos = jnp.arange(S, dtype=F32)
    inv_freq = ROPE_THETA ** (-(jnp.arange(0, MLA_ROPE, 2, dtype=F32) / MLA_ROPE))
    ang = pos[:, None] * inv_freq[None, :]
    cos, sin = jnp.cos(ang), jnp.sin(ang)
    pad = jnp.zeros((S, LANES - MLA_ROPE), F32)
    return jnp.concatenate([cos, cos, pad], axis=1), jnp.concatenate([-sin, sin, pad], axis=1)


ATTN_SLOTS = 2


def _tiles(S):
    t = lambda pref: min(pref, S)
    n_chunks = S // t(512)
    slots = min(ATTN_SLOTS, n_chunks)
    return dict(tm_in=t(512), tn_in=768, tm_pool=t(512), tm_mo=t(512), tm_q=t(512), q_heads=4,
                tk=t(512), tq=t(512), attn_group=min(16, n_chunks), attn_slots=slots, tm_out=t(512), tn_out=1024)


def _layer(x, p, cos_t, sin_t):
    S = x.shape[0]
    t = _tiles(S)
    u = _in_proj(x, p["norm_g"], p["w_in"], t["tm_in"], t["tn_in"])
    pool_o = _pool(u, p["pool_w"], p["pool_scale"], t["tm_pool"])
    hf, hb = _mlstm(u, p["bias_misc"])
    mlstm_o = _mlstm_out(hf, hb, u, p["mlstm_norm_g"], t["tm_mo"])
    qt = _q_up(u, p["qlat_g"], p["w_uq"], p["qn_g"], p["qr_g"], cos_t, sin_t, t["tm_q"], t["q_heads"])
    kf, vt = _kv_up(u, p["kvlat_g"], p["w_ukv"], p["kn_g"], p["kr_g"], cos_t, sin_t, t["tk"])
    mla_o = _attn(qt, kf, vt, u, t["tq"], t["attn_group"], t["attn_slots"])
    return _out_proj(pool_o, mlstm_o, mla_o, p["w_out"], x, t["tm_out"], t["tn_out"])


def _trunk(x, layers):
    x = x[0]
    cos_t, sin_t = _rope_tables(x.shape[0])
    for p in layers:
        x = _layer(x, p, cos_t, sin_t)
    return x[None]


def kernel(x_prompt, x_sample, norm_g, w_in, gate_bias, pool_w, pool_scale, mlstm_norm_g,
           qlat_g, w_uq, kvlat_g, w_ukv, qn_g, qr_g, kn_g, kr_g, w_out):
    params = (norm_g, w_in, gate_bias, pool_w, pool_scale, mlstm_norm_g,
              qlat_g, w_uq, kvlat_g, w_ukv, qn_g, qr_g, kn_g, kr_g, w_out)
    layers = [_prep_layer(*[w[l] for w in params]) for l in range(norm_g.shape[0])]
    return (_trunk(x_prompt, layers), _trunk(x_sample, layers))
```

```python
import functools

import numpy as np
import jax
import jax.numpy as jnp
from jax import lax
from jax.experimental import pallas as pl
from jax.experimental.pallas import tpu as pltpu

D_MODEL = 4096
D_POOL = 1024
D_MLSTM = 1024
D_MLA = 2048
POOL_WINDOWS = (2, 4, 8, 16)
POOL_GW = 256
MLSTM_HEADS = 4
MLSTM_HD = 256
MLSTM_CHUNK = 128
MLSTM_GATES = 16
MLA_HEADS = 16
MLA_V = 128
MLA_NOPE = 128
MLA_ROPE = 64
MLA_QK = 192
Q_LORA = 1536
KV_LORA = 512
ROPE_THETA = 10000.0
NORM_EPS = 1e-6
N_IN = 11344

LANES = 128
QK_PAD = 256
V_ONES = 16
V_AUG = MLA_V + V_ONES
LOG2_E = 1.4426950408889634
VMEM_LIMIT = 56 * 1024 * 1024

OFF_QLAT = 0
OFF_KVLAT = 1536
OFF_AZ = 2048
OFF_PX = 4096
OFF_PZ = 5120
OFF_MQ = 6144
OFF_MK = 7168
OFF_MV = 8192
OFF_MO = 9216
OFF_MZ = 10240
OFF_MISC = 11264
N_IN_PAD = 11520
MISC_GATE0 = MLA_ROPE

NEG_BIG = -1e30

F32 = jnp.float32
BF16 = jnp.bfloat16


def _cparams(sem):
    return pltpu.CompilerParams(dimension_semantics=sem, vmem_limit_bytes=VMEM_LIMIT)


def _sigmoid(x):
    return 1.0 / (1.0 + jnp.exp(-x))


def _silu(x):
    return x * _sigmoid(x)


def _dot(a, b):
    return jnp.dot(a, b, preferred_element_type=F32)


def _dot_nt(a, b):
    return lax.dot_general(a, b, (((1,), (1,)), ((), ())), preferred_element_type=F32)


def _dot_tn(a, b):
    return lax.dot_general(a, b, (((0,), (0,)), ((), ())), preferred_element_type=F32)


def _norm_matmul_kernel(x_ref, g_ref, w_ref, o_ref, h_ref):
    @pl.when(pl.program_id(1) == 0)
    def _():
        x = x_ref[...]
        ms = jnp.mean(x * x, axis=-1, keepdims=True)
        h_ref[...] = (x * lax.rsqrt(ms + NORM_EPS) * g_ref[...]).astype(BF16)

    o_ref[...] = _dot(h_ref[...], w_ref[...]).astype(o_ref.dtype)


def _in_proj(x, g, w, layer, tm, tn):
    S, K = x.shape
    N = w.shape[2]
    return pl.pallas_call(
        _norm_matmul_kernel,
        grid=(S // tm, N // tn),
        in_specs=[
            pl.BlockSpec((tm, K), lambda i, j: (i, 0)),
            pl.BlockSpec((1, K), lambda i, j: (0, 0)),
            pl.BlockSpec((None, K, tn), lambda i, j: (layer, 0, j)),
        ],
        out_specs=pl.BlockSpec((tm, tn), lambda i, j: (i, j)),
        out_shape=jax.ShapeDtypeStruct((S, N), F32),
        scratch_shapes=[pltpu.VMEM((tm, K), BF16)],
        compiler_params=_cparams(("parallel", "arbitrary")),
        name="in_proj",
    )(x, g, w)


POOL_HALO = 8


def _pool_kernel(prev_ref, cur_ref, next_ref, z_ref, w_ref, scale_ref, o_ref, *, seq_len):
    i = pl.program_id(0)
    tm = cur_ref.shape[0]
    rows = tm + 2 * POOL_HALO
    prev = jnp.where(i > 0, prev_ref[...], 0.0)
    nxt = jnp.where(i < pl.num_programs(0) - 1, next_ref[...], 0.0)
    cur = cur_ref[...]
    xall = jnp.concatenate([prev, cur, nxt], axis=0)
    pos = i * tm + lax.broadcasted_iota(jnp.int32, (tm, 1), 0)
    for g, win in enumerate(POOL_WINDOWS):
        half = win // 2
        sl = slice(g * POOL_GW, (g + 1) * POOL_GW)
        acc = xall[:, sl]
        span = 1
        while span < win:
            acc = acc + pltpu.roll(acc, span, axis=0)
            span *= 2
        shift = half - 1
        if shift:
            acc = pltpu.roll(acc, rows - shift, axis=0)
        wsum = acc[POOL_HALO:POOL_HALO + tm]
        cnt = jnp.minimum(pos + half, seq_len) - jnp.maximum(pos - half, 0)
        diff = wsum / cnt.astype(F32) - cur[:, sl]
        y = _dot(diff.astype(BF16), w_ref[g])
        o_ref[:, sl] = (y * scale_ref[:, sl] * _silu(z_ref[:, sl])).astype(o_ref.dtype)


def _pool(u, pool_w, pool_scale, tm):
    S = u.shape[0]
    hb = tm // POOL_HALO
    n_halo = S // POOL_HALO
    cpx = OFF_PX // D_POOL
    return pl.pallas_call(
        functools.partial(_pool_kernel, seq_len=S),
        grid=(S // tm,),
        in_specs=[
            pl.BlockSpec((POOL_HALO, D_POOL), lambda i: (jnp.maximum(i * hb - 1, 0), cpx)),
            pl.BlockSpec((tm, D_POOL), lambda i: (i, cpx)),
            pl.BlockSpec((POOL_HALO, D_POOL), lambda i: (jnp.minimum((i + 1) * hb, n_halo - 1), cpx)),
            pl.BlockSpec((tm, D_POOL), lambda i: (i, OFF_PZ // D_POOL)),
            pl.BlockSpec((len(POOL_WINDOWS), POOL_GW, POOL_GW), lambda i: (0, 0, 0)),
            pl.BlockSpec((1, D_POOL), lambda i: (0, 0)),
        ],
        out_specs=pl.BlockSpec((tm, D_POOL), lambda i: (i, 0)),
        out_shape=jax.ShapeDtypeStruct((S, D_POOL), BF16),
        compiler_params=_cparams(("parallel",)),
        name="pool",
    )(u, u, u, u, pool_w, pool_scale)


def _split3(x):
    hi = x.astype(BF16)
    r1 = x - hi.astype(F32)
    mid = r1.astype(BF16)
    lo = (r1 - mid.astype(F32)).astype(BF16)
    return hi, mid, lo


def _log_sigmoid(x):
    return jnp.minimum(x, 0.0) - jnp.log(1.0 + jnp.exp(-jnp.abs(x)))


def _lane_tile(x, reps):
    return jnp.concatenate([x] * reps, axis=1)


def _mlstm_chain(q, k, v, i_rep, b_rep, pm_rep, i_row, b_row, c_old, m, reverse):
    L = q.shape[0]
    row = lax.broadcasted_iota(jnp.int32, (L, L), 0)
    col = lax.broadcasted_iota(jnp.int32, (L, L), 1)
    keep = (col >= row) if reverse else (col <= row)
    b_last = b_rep[0:1] if reverse else b_rep[L - 1:L]

    m_t = b_rep + jnp.maximum(m, pm_rep)
    qb = q.astype(BF16)
    ks = k * (MLSTM_HD ** -0.5)
    v_aug = jnp.concatenate([v.astype(BF16), jnp.ones((L, LANES), BF16)], axis=1)
    dexp = jnp.exp(jnp.where(keep, (b_rep - m_t) - (b_row - i_row), NEG_BIG))
    w = dexp * _dot_nt(qb, ks.astype(BF16))
    ei = jnp.exp(b_rep + m - m_t)
    numden = (_lane_tile(ei, c_old.shape[1] // LANES) * _dot(qb, c_old.astype(BF16))
              + _dot(w.astype(BF16), v_aug))
    den = jnp.maximum(jnp.abs(numden[:, MLSTM_HD:]), jnp.exp(-m_t))
    h = numden[:, :MLSTM_HD] / _lane_tile(den, MLSTM_HD // LANES)

    a_rep = b_last - b_rep + i_rep
    m_new = jnp.maximum(b_last + m, jnp.max(a_rep, axis=0, keepdims=True))
    decay = jnp.exp(b_last + m - m_new)
    ek = _lane_tile(jnp.exp(a_rep - m_new), MLSTM_HD // LANES) * ks
    c_new = _lane_tile(decay, c_old.shape[1] // LANES) * c_old + _dot_tn(ek.astype(BF16), v_aug)
    return h, c_new, m_new


def _mlstm_kernel(qf_ref, kf_ref, vf_ref, gf_ref, qb_ref, kb_ref, vb_ref, gb_ref, bias_ref,
                  hf_ref, hb_ref, c_ref, m_ref):
    @pl.when(pl.program_id(0) == 0)
    def _():
        c_ref[...] = jnp.zeros_like(c_ref)
        m_ref[...] = jnp.zeros_like(m_ref)

    L = gf_ref.shape[0]
    row = lax.broadcasted_iota(jnp.int32, (L, L), 0)
    col = lax.broadcasted_iota(jnp.int32, (L, L), 1)
    dirs = ((qf_ref, kf_ref, vf_ref, gf_ref, hf_ref, False), (qb_ref, kb_ref, vb_ref, gb_ref, hb_ref, True))
    gates = []
    for d, (q_ref, k_ref, v_ref, g_ref, h_ref, reverse) in enumerate(dirs):
        g = g_ref[...] + bias_ref[...]
        lane_i0 = MISC_GATE0 + (2 * d) * MLSTM_HEADS
        tri = jnp.where((col >= row) if reverse else (col <= row), 1.0, 0.0).astype(BF16)
        b_all = sum(_dot(tri, p) for p in _split3(_log_sigmoid(g)))
        pm_all = g - pltpu.roll(b_all, LANES - MLSTM_HEADS, axis=1)
        shift = 1
        while shift < L:
            if reverse:
                moved = jnp.where(row < L - shift, pltpu.roll(pm_all, L - shift, axis=0), NEG_BIG)
            else:
                moved = jnp.where(row >= shift, pltpu.roll(pm_all, shift, axis=0), NEG_BIG)
            pm_all = jnp.maximum(pm_all, moved)
            shift *= 2
        gt = g.T
        bt = b_all.T
        for hd in range(MLSTM_HEADS):
            lane_i = lane_i0 + hd
            lane_f = lane_i0 + MLSTM_HEADS + hd
            lane_rep = lambda x, lane: jnp.broadcast_to(x[:, lane:lane + 1], (L, LANES))
            gates.append((lane_rep(g, lane_i), lane_rep(b_all, lane_f), lane_rep(pm_all, lane_i),
                          gt[lane_i:lane_i + 1, :], bt[lane_f:lane_f + 1, :]))
    results = []
    for d, (q_ref, k_ref, v_ref, g_ref, h_ref, reverse) in enumerate(dirs):
        for hd in range(MLSTM_HEADS):
            sl = slice(hd * MLSTM_HD, (hd + 1) * MLSTM_HD)
            idx = d * MLSTM_HEADS + hd
            h, c_new, m_new = _mlstm_chain(q_ref[:, sl], k_ref[:, sl], v_ref[:, sl], *gates[idx],
                                           c_ref[idx], m_ref[idx], reverse)
            results.append((h_ref, sl, idx, h, c_new, m_new))
    for h_ref, sl, idx, h, c_new, m_new in results:
        h_ref[:, sl] = h
        c_ref[idx] = c_new
        m_ref[idx] = m_new


def _mlstm(u, bias_misc):
    S = u.shape[0]
    L = MLSTM_CHUNK
    nc = S // L
    fwd = lambda c: c
    bwd = lambda c: nc - 1 - c

    def spec(off, width, cmap):
        return pl.BlockSpec((L, width), lambda c: (cmap(c), off // width))

    in_specs = []
    for cmap in (fwd, bwd):
        in_specs += [spec(OFF_MQ, D_MLSTM, cmap), spec(OFF_MK, D_MLSTM, cmap), spec(OFF_MV, D_MLSTM, cmap),
                     spec(OFF_MISC, LANES, cmap)]
    in_specs.append(pl.BlockSpec((1, LANES), lambda c: (0, 0)))
    nstate = 2 * MLSTM_HEADS
    return pl.pallas_call(
        _mlstm_kernel,
        grid=(nc,),
        in_specs=in_specs,
        out_specs=[pl.BlockSpec((L, D_MLSTM), lambda c: (c, 0)),
                   pl.BlockSpec((L, D_MLSTM), lambda c: (nc - 1 - c, 0))],
        out_shape=[jax.ShapeDtypeStruct((S, D_MLSTM), F32)] * 2,
        scratch_shapes=[pltpu.VMEM((nstate, MLSTM_HD, MLSTM_HD + LANES), F32),
                        pltpu.VMEM((nstate, 1, LANES), F32)],
        compiler_params=_cparams(("arbitrary",)),
        name="mlstm",
    )(u, u, u, u, u, u, u, u, bias_misc)


def _mlstm_out_kernel(hf_ref, hb_ref, o_ref, z_ref, g_ref, out_ref):
    for hd in range(MLSTM_HEADS):
        sl = slice(hd * MLSTM_HD, (hd + 1) * MLSTM_HD)
        h = hf_ref[:, sl] + hb_ref[:, sl]
        ms = jnp.mean(h * h, axis=-1, keepdims=True)
        hn = h * lax.rsqrt(ms + NORM_EPS) * g_ref[:, sl]
        out_ref[:, sl] = (_sigmoid(o_ref[:, sl]) * hn * _silu(z_ref[:, sl])).astype(out_ref.dtype)


def _mlstm_out(hf, hb, u, norm_g, tm):
    S = hf.shape[0]
    return pl.pallas_call(
        _mlstm_out_kernel,
        grid=(S // tm,),
        in_specs=[
            pl.BlockSpec((tm, D_MLSTM), lambda i: (i, 0)),
            pl.BlockSpec((tm, D_MLSTM), lambda i: (i, 0)),
            pl.BlockSpec((tm, D_MLSTM), lambda i: (i, OFF_MO // D_MLSTM)),
            pl.BlockSpec((tm, D_MLSTM), lambda i: (i, OFF_MZ // D_MLSTM)),
            pl.BlockSpec((1, D_MLSTM), lambda i: (0, 0)),
        ],
        out_specs=pl.BlockSpec((tm, D_MLSTM), lambda i: (i, 0)),
        out_shape=jax.ShapeDtypeStruct((S, D_MLSTM), BF16),
        compiler_params=_cparams(("parallel",)),
        name="mlstm_out",
    )(hf, hb, u, u, norm_g)


def _rope128(x, g, cos_ref, sin_ref):
    lane = lax.broadcasted_iota(jnp.int32, x.shape, 1)
    x = jnp.where(lane < MLA_ROPE, x, 0.0)
    ms = jnp.sum(x * x, axis=-1, keepdims=True) * (1.0 / MLA_ROPE)
    xn = x * lax.rsqrt(ms + NORM_EPS) * g
    half = MLA_ROPE // 2
    partner = jnp.where(lane < half, pltpu.roll(xn, LANES - half, axis=1), pltpu.roll(xn, half, axis=1))
    return xn * cos_ref[...] + partner * sin_ref[...]


def _q_up_kernel(x_ref, g_ref, w_ref, qn_ref, qr_ref, cos_ref, sin_ref, o_ref, h_ref, acc_ref,
                 *, heads_per_tile, n_col, n_tiles):
    s = pl.program_id(0)

    @pl.when(jnp.logical_and(s % n_col == 0, s < n_tiles))
    def _():
        x = x_ref[...]
        ms = jnp.mean(x * x, axis=-1, keepdims=True)
        h_ref[...] = (x * lax.rsqrt(ms + NORM_EPS) * g_ref[...]).astype(BF16)

    @pl.when(s == 0)
    def _():
        acc_ref[1] = jnp.zeros(acc_ref.shape[1:], acc_ref.dtype)

    scale = MLA_QK ** -0.5 * LOG2_E

    def step(fill, drain):
        for hh in range(heads_per_tile):
            cols = slice(hh * QK_PAD, (hh + 1) * QK_PAD)
            acc_ref[fill, :, cols] = _dot(h_ref[...], w_ref[:, cols])
        for hh in range(heads_per_tile):
            nope = acc_ref[drain, :, hh * QK_PAD: hh * QK_PAD + MLA_NOPE]
            ms = jnp.mean(nope * nope, axis=-1, keepdims=True)
            nope = nope * lax.rsqrt(ms + NORM_EPS) * qn_ref[...]
            pe = _rope128(acc_ref[drain, :, hh * QK_PAD + MLA_NOPE: (hh + 1) * QK_PAD], qr_ref[...], cos_ref, sin_ref)
            o_ref[hh * QK_PAD: hh * QK_PAD + MLA_NOPE, :] = (nope * scale).astype(o_ref.dtype).T
            o_ref[hh * QK_PAD + MLA_NOPE: (hh + 1) * QK_PAD, :] = (pe * scale).astype(o_ref.dtype).T

    @pl.when(s % 2 == 0)
    def _():
        step(0, 1)

    @pl.when(s % 2 == 1)
    def _():
        step(1, 0)


def _q_up(u, g, w, layer, qn_g, qr_g, cos_t, sin_t, tm, heads_per_tile):
    S = u.shape[0]
    tn = heads_per_tile * QK_PAD
    N = MLA_HEADS * QK_PAD
    n_col = N // tn
    n_tiles = (S // tm) * n_col
    fill_row = lambda s: jnp.minimum(s, n_tiles - 1) // n_col
    fill_col = lambda s: jnp.minimum(s, n_tiles - 1) % n_col
    drain_row = lambda s: jnp.maximum(s - 1, 0) // n_col
    drain_col = lambda s: jnp.maximum(s - 1, 0) % n_col
    return pl.pallas_call(
        functools.partial(_q_up_kernel, heads_per_tile=heads_per_tile, n_col=n_col, n_tiles=n_tiles),
        grid=(n_tiles + 1,),
        in_specs=[
            pl.BlockSpec((tm, Q_LORA), lambda s: (fill_row(s), OFF_QLAT // Q_LORA)),
            pl.BlockSpec((1, Q_LORA), lambda s: (0, 0)),
            pl.BlockSpec((None, Q_LORA, tn), lambda s: (layer, 0, fill_col(s))),
            pl.BlockSpec((1, LANES), lambda s: (0, 0)),
            pl.BlockSpec((1, LANES), lambda s: (0, 0)),
            pl.BlockSpec((tm, LANES), lambda s: (drain_row(s), 0)),
            pl.BlockSpec((tm, LANES), lambda s: (drain_row(s), 0)),
        ],
        out_specs=pl.BlockSpec((tn, tm), lambda s: (drain_col(s), drain_row(s))),
        out_shape=jax.ShapeDtypeStruct((N, S), BF16),
        scratch_shapes=[pltpu.VMEM((tm, Q_LORA), BF16), pltpu.VMEM((2, tm, tn), F32)],
        compiler_params=_cparams(("arbitrary",)),
        name="q_up",
    )(u, g, w, qn_g, qr_g, cos_t, sin_t)


def _kv_up_kernel(x_ref, misc_ref, g_ref, w_ref, kn_ref, kr_ref, cos_ref, sin_ref, k_ref, vt_ref):
    x = x_ref[...]
    ms = jnp.mean(x * x, axis=-1, keepdims=True)
    h = (x * lax.rsqrt(ms + NORM_EPS) * g_ref[...]).astype(BF16)
    acc = _dot(h, w_ref[...])
    pe = _rope128(misc_ref[...], kr_ref[...], cos_ref, sin_ref).astype(k_ref.dtype)
    for hd in range(MLA_HEADS):
        nope = acc[:, hd * MLA_NOPE: (hd + 1) * MLA_NOPE]
        ms = jnp.mean(nope * nope, axis=-1, keepdims=True)
        k_ref[:, hd * QK_PAD: hd * QK_PAD + MLA_NOPE] = (nope * lax.rsqrt(ms + NORM_EPS) * kn_ref[...]).astype(k_ref.dtype)
        k_ref[:, hd * QK_PAD + MLA_NOPE: (hd + 1) * QK_PAD] = pe
        v = acc[:, MLA_HEADS * MLA_NOPE + hd * MLA_V: MLA_HEADS * MLA_NOPE + (hd + 1) * MLA_V]
        vt_ref[hd * V_AUG: hd * V_AUG + MLA_V, :] = v.astype(vt_ref.dtype).T
        vt_ref[hd * V_AUG + MLA_V: (hd + 1) * V_AUG, :] = jnp.ones((V_ONES, vt_ref.shape[1]), vt_ref.dtype)


def _kv_up(u, g, w, kn_g, kr_g, cos_t, sin_t, tk):
    S = u.shape[0]
    return pl.pallas_call(
        _kv_up_kernel,
        grid=(S // tk,),
        in_specs=[
            pl.BlockSpec((tk, KV_LORA), lambda i: (i, OFF_KVLAT // KV_LORA)),
            pl.BlockSpec((tk, LANES), lambda i: (i, OFF_MISC // LANES)),
            pl.BlockSpec((1, KV_LORA), lambda i: (0, 0)),
            pl.BlockSpec((KV_LORA, 2 * D_MLA), lambda i: (0, 0)),
            pl.BlockSpec((1, LANES), lambda i: (0, 0)),
            pl.BlockSpec((1, LANES), lambda i: (0, 0)),
            pl.BlockSpec((tk, LANES), lambda i: (i, 0)),
            pl.BlockSpec((tk, LANES), lambda i: (i, 0)),
        ],
        out_specs=[pl.BlockSpec((tk, MLA_HEADS * QK_PAD), lambda i: (i, 0)),
                   pl.BlockSpec((None, MLA_HEADS * V_AUG, tk), lambda i: (i, 0, 0))],
        out_shape=[jax.ShapeDtypeStruct((S, MLA_HEADS * QK_PAD), BF16),
                   jax.ShapeDtypeStruct((S // tk, MLA_HEADS * V_AUG, tk), BF16)],
        compiler_params=_cparams(("parallel",)),
        name="kv_up",
    )(u, u, g, w, kn_g, kr_g, cos_t, sin_t)


def _attn_kernel(qt_ref, qnext_ref, k_ref, vt_ref, z_ref, o_ref, s_ref, acc_ref, cmax_ref, *, group_size):
    n_chunks, _, tk = vt_ref.shape
    tq = qt_ref.shape[1]
    n_slots = s_ref.shape[0]

    def scores(q_ref, j, slot):
        k = k_ref[pl.ds(pl.multiple_of(j * tk, tk), tk), :]
        st = _dot(k, q_ref[...])
        s_ref[slot] = st
        return jnp.max(st, axis=0, keepdims=True)

    def accumulate(j, slot, m, cmax):
        m_new = jnp.maximum(m, cmax)
        alpha = jnp.exp2(m - m_new)
        p = jnp.exp2(s_ref[slot] - m_new).astype(BF16)
        acc_ref[...] = alpha * acc_ref[...] + _dot(vt_ref[j], p)
        return m_new

    def group(j0, m, cmax, last):
        for u in range(group_size):
            if last and u == group_size - 1:
                cnext = scores(qnext_ref, 0, 0)
            else:
                cnext = scores(qt_ref, j0 + u + 1, (u + 1) % n_slots)
            m = accumulate(j0 + u, u % n_slots, m, cmax)
            cmax = cnext
        return m, cmax

    @pl.when(pl.program_id(1) == 0)
    def _():
        cmax_ref[...] = scores(qt_ref, 0, 0)

    acc_ref[...] = jnp.zeros_like(acc_ref)
    m = jnp.full((1, tq), NEG_BIG, F32)
    n_groups = n_chunks // group_size
    m, cmax = lax.fori_loop(0, n_groups - 1, lambda g, c: group(g * group_size, c[0], c[1], False),
                            (m, cmax_ref[...]))
    _, cmax_ref[...] = group((n_groups - 1) * group_size, m, cmax, True)
    o = (acc_ref[0:MLA_V, :] / acc_ref[MLA_V:MLA_V + 1, :]).T
    o_ref[...] = (o * _silu(z_ref[...])).astype(o_ref.dtype)


def _attn(qt, kf, vt, u, tq, group_size, n_slots):
    S = kf.shape[0]
    n_chunks, _, tk = vt.shape
    n_q = S // tq
    assert n_chunks % group_size == 0 and group_size % n_slots == 0 and n_slots >= 2
    return pl.pallas_call(
        functools.partial(_attn_kernel, group_size=group_size),
        grid=(MLA_HEADS, n_q),
        in_specs=[
            pl.BlockSpec((QK_PAD, tq), lambda h, i: (h, i)),
            pl.BlockSpec((QK_PAD, tq), lambda h, i: (h, jnp.minimum(i + 1, n_q - 1))),
            pl.BlockSpec((S, QK_PAD), lambda h, i: (0, h)),
            pl.BlockSpec((n_chunks, V_AUG, tk), lambda h, i: (0, h, 0)),
            pl.BlockSpec((tq, MLA_V), lambda h, i: (i, OFF_AZ // MLA_V + h)),
        ],
        out_specs=pl.BlockSpec((tq, MLA_V), lambda h, i: (i, h)),
        out_shape=jax.ShapeDtypeStruct((S, D_MLA), BF16),
        scratch_shapes=[pltpu.VMEM((n_slots, tk, tq), F32), pltpu.VMEM((V_AUG, tq), F32),
                        pltpu.VMEM((1, tq), F32)],
        compiler_params=_cparams(("parallel", "arbitrary")),
        name="attn",
    )(qt, qt, kf, vt, u)


def _out_proj_kernel(p_ref, m_ref, a_ref, w_ref, x_ref, o_ref):
    acc = _dot(p_ref[...], w_ref[0:D_POOL, :])
    acc += _dot(m_ref[...], w_ref[D_POOL:D_POOL + D_MLSTM, :])
    acc += _dot(a_ref[...], w_ref[D_POOL + D_MLSTM:, :])
    o_ref[...] = x_ref[...] + acc


def _out_proj(pool_o, mlstm_o, mla_o, w, layer, x, tm, tn):
    S = x.shape[0]
    return pl.pallas_call(
        _out_proj_kernel,
        grid=(S // tm, D_MODEL // tn),
        in_specs=[
            pl.BlockSpec((tm, D_POOL), lambda i, j: (i, 0)),
            pl.BlockSpec((tm, D_MLSTM), lambda i, j: (i, 0)),
            pl.BlockSpec((tm, D_MLA), lambda i, j: (i, 0)),
            pl.BlockSpec((None, D_MODEL, tn), lambda i, j: (layer, 0, j)),
            pl.BlockSpec((tm, tn), lambda i, j: (i, j)),
        ],
        out_specs=pl.BlockSpec((tm, tn), lambda i, j: (i, j)),
        out_shape=jax.ShapeDtypeStruct((S, D_MODEL), F32),
        compiler_params=_cparams(("parallel", "arbitrary")),
        name="out_proj",
    )(pool_o, mlstm_o, mla_o, w, x)


def _pad_lanes(v, offset=0):
    return jnp.zeros((1, LANES), F32).at[0, offset:offset + v.shape[0]].set(v.astype(F32))


_SRC_SIZES = (D_POOL, D_POOL, D_MLSTM, D_MLSTM, D_MLSTM, D_MLSTM, D_MLSTM, MLSTM_GATES,
              Q_LORA, KV_LORA, MLA_ROPE, D_MLA)
(_SRC_PX, _, _, _, _, _, _, _SRC_MG, _SRC_QLAT, _SRC_KVLAT, _SRC_KROPE, _SRC_AZ) = (
    int(o) for o in np.concatenate([[0], np.cumsum(_SRC_SIZES)])[:-1])
W_IN_MOVES = (
    (OFF_QLAT, _SRC_QLAT, Q_LORA), (OFF_KVLAT, _SRC_KVLAT, KV_LORA), (OFF_AZ, _SRC_AZ, D_MLA),
    (OFF_PX, _SRC_PX, 2 * D_POOL + 5 * D_MLSTM),
    (OFF_MISC, _SRC_KROPE, MLA_ROPE), (OFF_MISC + MISC_GATE0, _SRC_MG, MLSTM_GATES),
)
W_IN_USED = OFF_MISC + MISC_GATE0 + MLSTM_GATES


def _w_in_prep_kernel(w_ref, o_ref):
    for dst, src, width in W_IN_MOVES:
        o_ref[:, dst:dst + width] = w_ref[:, src:src + width].astype(o_ref.dtype)
    o_ref[:, W_IN_USED:] = jnp.zeros((o_ref.shape[0], N_IN_PAD - W_IN_USED), o_ref.dtype)


def _w_uq_prep_kernel(w_ref, o_ref):
    for hd in range(MLA_HEADS):
        o_ref[:, hd * QK_PAD: hd * QK_PAD + MLA_QK] = w_ref[:, hd * MLA_QK: (hd + 1) * MLA_QK].astype(o_ref.dtype)
        o_ref[:, hd * QK_PAD + MLA_QK: (hd + 1) * QK_PAD] = jnp.zeros((o_ref.shape[0], QK_PAD - MLA_QK), o_ref.dtype)


def _prep_weight(body, w, n_out, tr, name):
    n_layers, K, N = w.shape
    return pl.pallas_call(
        body,
        grid=(n_layers, K // tr),
        in_specs=[pl.BlockSpec((None, tr, N), lambda l, i: (l, i, 0))],
        out_specs=pl.BlockSpec((None, tr, n_out), lambda l, i: (l, i, 0)),
        out_shape=jax.ShapeDtypeStruct((n_layers, K, n_out), BF16),
        compiler_params=_cparams(("parallel", "parallel")),
        name=name,
    )(w)


def _prep_layer(norm_g, gate_bias, pool_w, pool_scale, mlstm_norm_g, qlat_g, kvlat_g, w_ukv, qn_g, qr_g, kn_g, kr_g):
    wkv = w_ukv.reshape(KV_LORA, MLA_HEADS, MLA_NOPE + MLA_V)
    w_ukv_p = jnp.concatenate([wkv[:, :, :MLA_NOPE].reshape(KV_LORA, -1),
                               wkv[:, :, MLA_NOPE:].reshape(KV_LORA, -1)], axis=1).astype(BF16)
    return dict(
        norm_g=norm_g.reshape(1, D_MODEL),
        bias_misc=_pad_lanes(gate_bias, MISC_GATE0),
        pool_w=pool_w.astype(BF16), pool_scale=pool_scale.reshape(1, D_POOL),
        mlstm_norm_g=mlstm_norm_g.reshape(1, D_MLSTM),
        qlat_g=qlat_g.reshape(1, Q_LORA),
        kvlat_g=kvlat_g.reshape(1, KV_LORA), w_ukv=w_ukv_p,
        qn_g=qn_g.reshape(1, MLA_NOPE), qr_g=_pad_lanes(qr_g),
        kn_g=kn_g.reshape(1, MLA_NOPE), kr_g=_pad_lanes(kr_g),
    )


def _rope_tables(S):
    pos = jnp.arange(S, dtype=F32)
    inv_freq = ROPE_THETA ** (-(jnp.arange(0, MLA_ROPE, 2, dtype=F32) / MLA_ROPE))
    ang = pos[:, None] * inv_freq[None, :]
    cos, sin = jnp.cos(ang), jnp.sin(ang)
    pad = jnp.zeros((S, LANES - MLA_ROPE), F32)
    return jnp.concatenate([cos, cos, pad], axis=1), jnp.concatenate([-sin, sin, pad], axis=1)


ATTN_SLOTS = 2


def _tiles(S):
    t = lambda pref: min(pref, S)
    n_chunks = S // t(512)
    slots = min(ATTN_SLOTS, n_chunks)
    return dict(tm_in=t(512), tn_in=768, tm_pool=t(512), tm_mo=t(512), tm_q=t(512), q_heads=4,
                tk=t(512), tq=t(512), attn_group=min(16, n_chunks), attn_slots=slots, tm_out=t(512), tn_out=1024)


def _layer(x, layer, p, big, cos_t, sin_t):
    S = x.shape[0]
    t = _tiles(S)
    u = _in_proj(x, p["norm_g"], big["w_in"], layer, t["tm_in"], t["tn_in"])
    pool_o = _pool(u, p["pool_w"], p["pool_scale"], t["tm_pool"])
    hf, hb = _mlstm(u, p["bias_misc"])
    mlstm_o = _mlstm_out(hf, hb, u, p["mlstm_norm_g"], t["tm_mo"])
    qt = _q_up(u, p["qlat_g"], big["w_uq"], layer, p["qn_g"], p["qr_g"], cos_t, sin_t, t["tm_q"], t["q_heads"])
    kf, vt = _kv_up(u, p["kvlat_g"], p["w_ukv"], p["kn_g"], p["kr_g"], cos_t, sin_t, t["tk"])
    mla_o = _attn(qt, kf, vt, u, t["tq"], t["attn_group"], t["attn_slots"])
    return _out_proj(pool_o, mlstm_o, mla_o, big["w_out"], layer, x, t["tm_out"], t["tn_out"])


def _trunk(x, layers, big):
    x = x[0]
    cos_t, sin_t = _rope_tables(x.shape[0])
    for layer, p in enumerate(layers):
        x = _layer(x, layer, p, big, cos_t, sin_t)
    return x[None]


def kernel(x_prompt, x_sample, norm_g, w_in, gate_bias, pool_w, pool_scale, mlstm_norm_g,
           qlat_g, w_uq, kvlat_g, w_ukv, qn_g, qr_g, kn_g, kr_g, w_out):
    small = (norm_g, gate_bias, pool_w, pool_scale, mlstm_norm_g, qlat_g, kvlat_g, w_ukv, qn_g, qr_g, kn_g, kr_g)
    layers = [_prep_layer(*[w[l] for w in small]) for l in range(norm_g.shape[0])]
    big = dict(w_in=_prep_weight(_w_in_prep_kernel, w_in, N_IN_PAD, 256, "w_in_prep"),
               w_uq=_prep_weight(_w_uq_prep_kernel, w_uq, MLA_HEADS * QK_PAD, 512, "w_uq_prep"),
               w_out=w_out.astype(BF16))
    return (_trunk(x_prompt, layers, big), _trunk(x_sample, layers, big))
```

```python
import functools

import numpy as np
import jax
import jax.numpy as jnp
from jax import lax
from jax.experimental import pallas as pl
from jax.experimental.pallas import tpu as pltpu

D_MODEL = 4096
D_POOL = 1024
D_MLSTM = 1024
D_MLA = 2048
POOL_WINDOWS = (2, 4, 8, 16)
POOL_GW = 256
MLSTM_HEADS = 4
MLSTM_HD = 256
MLSTM_CHUNK = 128
MLSTM_GATES = 16
MLA_HEADS = 16
MLA_V = 128
MLA_NOPE = 128
MLA_ROPE = 64
MLA_QK = 192
Q_LORA = 1536
KV_LORA = 512
ROPE_THETA = 10000.0
NORM_EPS = 1e-6
N_IN = 11344

LANES = 128
QK_PAD = 256
V_ONES = 16
V_AUG = MLA_V + V_ONES
LOG2_E = 1.4426950408889634
VMEM_LIMIT = 56 * 1024 * 1024

OFF_QLAT = 0
OFF_KVLAT = 1536
OFF_AZ = 2048
OFF_PX = 4096
OFF_PZ = 5120
OFF_MQ = 6144
OFF_MK = 7168
OFF_MV = 8192
OFF_MO = 9216
OFF_MZ = 10240
OFF_MISC = 11264
N_IN_PAD = 11520
MISC_GATE0 = MLA_ROPE

NEG_BIG = -1e30

F32 = jnp.float32
BF16 = jnp.bfloat16


def _cparams(sem):
    return pltpu.CompilerParams(dimension_semantics=sem, vmem_limit_bytes=VMEM_LIMIT)


def _sigmoid(x):
    return 1.0 / (1.0 + jnp.exp(-x))


def _silu(x):
    return x * _sigmoid(x)


def _dot(a, b):
    return jnp.dot(a, b, preferred_element_type=F32)


def _dot_nt(a, b):
    return lax.dot_general(a, b, (((1,), (1,)), ((), ())), preferred_element_type=F32)


def _dot_tn(a, b):
    return lax.dot_general(a, b, (((0,), (0,)), ((), ())), preferred_element_type=F32)


def _norm_matmul_kernel(x_ref, g_ref, w_ref, o_ref, h_ref):
    @pl.when(pl.program_id(1) == 0)
    def _():
        x = x_ref[...]
        ms = jnp.mean(x * x, axis=-1, keepdims=True)
        h_ref[...] = (x * lax.rsqrt(ms + NORM_EPS) * g_ref[...]).astype(BF16)

    o_ref[...] = _dot(h_ref[...], w_ref[...]).astype(o_ref.dtype)


def _in_proj(x, g, w, layer, tm, tn):
    S, K = x.shape
    N = w.shape[2]
    return pl.pallas_call(
        _norm_matmul_kernel,
        grid=(S // tm, N // tn),
        in_specs=[
            pl.BlockSpec((tm, K), lambda i, j: (i, 0)),
            pl.BlockSpec((1, K), lambda i, j: (0, 0)),
            pl.BlockSpec((None, K, tn), lambda i, j: (layer, 0, j)),
        ],
        out_specs=pl.BlockSpec((tm, tn), lambda i, j: (i, j)),
        out_shape=jax.ShapeDtypeStruct((S, N), F32),
        scratch_shapes=[pltpu.VMEM((tm, K), BF16)],
        compiler_params=_cparams(("parallel", "arbitrary")),
        name="in_proj",
    )(x, g, w)


POOL_HALO = 8


def _pool_kernel(prev_ref, cur_ref, next_ref, z_ref, w_ref, scale_ref, o_ref, *, seq_len):
    i = pl.program_id(0)
    tm = cur_ref.shape[0]
    rows = tm + 2 * POOL_HALO
    prev = jnp.where(i > 0, prev_ref[...], 0.0)
    nxt = jnp.where(i < pl.num_programs(0) - 1, next_ref[...], 0.0)
    cur = cur_ref[...]
    xall = jnp.concatenate([prev, cur, nxt], axis=0)
    pos = i * tm + lax.broadcasted_iota(jnp.int32, (tm, 1), 0)
    for g, win in enumerate(POOL_WINDOWS):
        half = win // 2
        sl = slice(g * POOL_GW, (g + 1) * POOL_GW)
        acc = xall[:, sl]
        span = 1
        while span < win:
            acc = acc + pltpu.roll(acc, span, axis=0)
            span *= 2
        shift = half - 1
        if shift:
            acc = pltpu.roll(acc, rows - shift, axis=0)
        wsum = acc[POOL_HALO:POOL_HALO + tm]
        cnt = jnp.minimum(pos + half, seq_len) - jnp.maximum(pos - half, 0)
        diff = wsum / cnt.astype(F32) - cur[:, sl]
        y = _dot(diff.astype(BF16), w_ref[g])
        o_ref[:, sl] = (y * scale_ref[:, sl] * _silu(z_ref[:, sl])).astype(o_ref.dtype)


def _pool(u, pool_w, pool_scale, tm):
    S = u.shape[0]
    hb = tm // POOL_HALO
    n_halo = S // POOL_HALO
    cpx = OFF_PX // D_POOL
    return pl.pallas_call(
        functools.partial(_pool_kernel, seq_len=S),
        grid=(S // tm,),
        in_specs=[
            pl.BlockSpec((POOL_HALO, D_POOL), lambda i: (jnp.maximum(i * hb - 1, 0), cpx)),
            pl.BlockSpec((tm, D_POOL), lambda i: (i, cpx)),
            pl.BlockSpec((POOL_HALO, D_POOL), lambda i: (jnp.minimum((i + 1) * hb, n_halo - 1), cpx)),
            pl.BlockSpec((tm, D_POOL), lambda i: (i, OFF_PZ // D_POOL)),
            pl.BlockSpec((len(POOL_WINDOWS), POOL_GW, POOL_GW), lambda i: (0, 0, 0)),
            pl.BlockSpec((1, D_POOL), lambda i: (0, 0)),
        ],
        out_specs=pl.BlockSpec((tm, D_POOL), lambda i: (i, 0)),
        out_shape=jax.ShapeDtypeStruct((S, D_POOL), BF16),
        compiler_params=_cparams(("parallel",)),
        name="pool",
    )(u, u, u, u, pool_w, pool_scale)


def _split3(x):
    hi = x.astype(BF16)
    r1 = x - hi.astype(F32)
    mid = r1.astype(BF16)
    lo = (r1 - mid.astype(F32)).astype(BF16)
    return hi, mid, lo


def _log_sigmoid(x):
    return jnp.minimum(x, 0.0) - jnp.log(1.0 + jnp.exp(-jnp.abs(x)))


def _lane_tile(x, reps):
    return jnp.concatenate([x] * reps, axis=1)


def _mlstm_chain(q, k, v, i_rep, b_rep, pm_rep, i_row, b_row, c_old, m, reverse):
    L = q.shape[0]
    row = lax.broadcasted_iota(jnp.int32, (L, L), 0)
    col = lax.broadcasted_iota(jnp.int32, (L, L), 1)
    keep = (col >= row) if reverse else (col <= row)
    b_last = b_rep[0:1] if reverse else b_rep[L - 1:L]

    m_t = b_rep + jnp.maximum(m, pm_rep)
    qb = q.astype(BF16)
    ks = k * (MLSTM_HD ** -0.5)
    v_aug = jnp.concatenate([v.astype(BF16), jnp.ones((L, LANES), BF16)], axis=1)
    dexp = jnp.exp(jnp.where(keep, (b_rep - m_t) - (b_row - i_row), NEG_BIG))
    w = dexp * _dot_nt(qb, ks.astype(BF16))
    ei = jnp.exp(b_rep + m - m_t)
    numden = (_lane_tile(ei, c_old.shape[1] // LANES) * _dot(qb, c_old.astype(BF16))
              + _dot(w.astype(BF16), v_aug))
    den = jnp.maximum(jnp.abs(numden[:, MLSTM_HD:]), jnp.exp(-m_t))
    h = numden[:, :MLSTM_HD] / _lane_tile(den, MLSTM_HD // LANES)

    a_rep = b_last - b_rep + i_rep
    m_new = jnp.maximum(b_last + m, jnp.max(a_rep, axis=0, keepdims=True))
    decay = jnp.exp(b_last + m - m_new)
    ek = _lane_tile(jnp.exp(a_rep - m_new), MLSTM_HD // LANES) * ks
    c_new = _lane_tile(decay, c_old.shape[1] // LANES) * c_old + _dot_tn(ek.astype(BF16), v_aug)
    return h, c_new, m_new


def _mlstm_kernel(qf_ref, kf_ref, vf_ref, gf_ref, qb_ref, kb_ref, vb_ref, gb_ref, bias_ref,
                  hf_ref, hb_ref, c_ref, m_ref):
    @pl.when(pl.program_id(0) == 0)
    def _():
        c_ref[...] = jnp.zeros_like(c_ref)
        m_ref[...] = jnp.zeros_like(m_ref)

    L = gf_ref.shape[0]
    row = lax.broadcasted_iota(jnp.int32, (L, L), 0)
    col = lax.broadcasted_iota(jnp.int32, (L, L), 1)
    dirs = ((qf_ref, kf_ref, vf_ref, gf_ref, hf_ref, False), (qb_ref, kb_ref, vb_ref, gb_ref, hb_ref, True))
    gates = []
    for d, (q_ref, k_ref, v_ref, g_ref, h_ref, reverse) in enumerate(dirs):
        g = g_ref[...] + bias_ref[...]
        lane_i0 = MISC_GATE0 + (2 * d) * MLSTM_HEADS
        tri = jnp.where((col >= row) if reverse else (col <= row), 1.0, 0.0).astype(BF16)
        b_all = sum(_dot(tri, p) for p in _split3(_log_sigmoid(g)))
        pm_all = g - pltpu.roll(b_all, LANES - MLSTM_HEADS, axis=1)
        shift = 1
        while shift < L:
            if reverse:
                moved = jnp.where(row < L - shift, pltpu.roll(pm_all, L - shift, axis=0), NEG_BIG)
            else:
                moved = jnp.where(row >= shift, pltpu.roll(pm_all, shift, axis=0), NEG_BIG)
            pm_all = jnp.maximum(pm_all, moved)
            shift *= 2
        gt = g.T
        bt = b_all.T
        for hd in range(MLSTM_HEADS):
            lane_i = lane_i0 + hd
            lane_f = lane_i0 + MLSTM_HEADS + hd
            lane_rep = lambda x, lane: jnp.broadcast_to(x[:, lane:lane + 1], (L, LANES))
            gates.append((lane_rep(g, lane_i), lane_rep(b_all, lane_f), lane_rep(pm_all, lane_i),
                          gt[lane_i:lane_i + 1, :], bt[lane_f:lane_f + 1, :]))
    results = []
    for d, (q_ref, k_ref, v_ref, g_ref, h_ref, reverse) in enumerate(dirs):
        for hd in range(MLSTM_HEADS):
            sl = slice(hd * MLSTM_HD, (hd + 1) * MLSTM_HD)
            idx = d * MLSTM_HEADS + hd
            h, c_new, m_new = _mlstm_chain(q_ref[:, sl], k_ref[:, sl], v_ref[:, sl], *gates[idx],
                                           c_ref[idx], m_ref[idx], reverse)
            results.append((h_ref, sl, idx, h, c_new, m_new))
    for h_ref, sl, idx, h, c_new, m_new in results:
        h_ref[:, sl] = h
        c_ref[idx] = c_new
        m_ref[idx] = m_new


def _mlstm(u, bias_misc):
    S = u.shape[0]
    L = MLSTM_CHUNK
    nc = S // L
    fwd = lambda c: c
    bwd = lambda c: nc - 1 - c

    def spec(off, width, cmap):
        return pl.BlockSpec((L, width), lambda c: (cmap(c), off // width))

    in_specs = []
    for cmap in (fwd, bwd):
        in_specs += [spec(OFF_MQ, D_MLSTM, cmap), spec(OFF_MK, D_MLSTM, cmap), spec(OFF_MV, D_MLSTM, cmap),
                     spec(OFF_MISC, LANES, cmap)]
    in_specs.append(pl.BlockSpec((1, LANES), lambda c: (0, 0)))
    nstate = 2 * MLSTM_HEADS
    return pl.pallas_call(
        _mlstm_kernel,
        grid=(nc,),
        in_specs=in_specs,
        out_specs=[pl.BlockSpec((L, D_MLSTM), lambda c: (c, 0)),
                   pl.BlockSpec((L, D_MLSTM), lambda c: (nc - 1 - c, 0))],
        out_shape=[jax.ShapeDtypeStruct((S, D_MLSTM), F32)] * 2,
        scratch_shapes=[pltpu.VMEM((nstate, MLSTM_HD, MLSTM_HD + LANES), F32),
                        pltpu.VMEM((nstate, 1, LANES), F32)],
        compiler_params=_cparams(("arbitrary",)),
        name="mlstm",
    )(u, u, u, u, u, u, u, u, bias_misc)


def _mlstm_out_kernel(hf_ref, hb_ref, o_ref, z_ref, g_ref, out_ref):
    for hd in range(MLSTM_HEADS):
        sl = slice(hd * MLSTM_HD, (hd + 1) * MLSTM_HD)
        h = hf_ref[:, sl] + hb_ref[:, sl]
        ms = jnp.mean(h * h, axis=-1, keepdims=True)
        hn = h * lax.rsqrt(ms + NORM_EPS) * g_ref[:, sl]
        out_ref[:, sl] = (_sigmoid(o_ref[:, sl]) * hn * _silu(z_ref[:, sl])).astype(out_ref.dtype)


def _mlstm_out(hf, hb, u, norm_g, tm):
    S = hf.shape[0]
    return pl.pallas_call(
        _mlstm_out_kernel,
        grid=(S // tm,),
        in_specs=[
            pl.BlockSpec((tm, D_MLSTM), lambda i: (i, 0)),
            pl.BlockSpec((tm, D_MLSTM), lambda i: (i, 0)),
            pl.BlockSpec((tm, D_MLSTM), lambda i: (i, OFF_MO // D_MLSTM)),
            pl.BlockSpec((tm, D_MLSTM), lambda i: (i, OFF_MZ // D_MLSTM)),
            pl.BlockSpec((1, D_MLSTM), lambda i: (0, 0)),
        ],
        out_specs=pl.BlockSpec((tm, D_MLSTM), lambda i: (i, 0)),
        out_shape=jax.ShapeDtypeStruct((S, D_MLSTM), BF16),
        compiler_params=_cparams(("parallel",)),
        name="mlstm_out",
    )(hf, hb, u, u, norm_g)


def _rope128(x, g, cos_ref, sin_ref):
    lane = lax.broadcasted_iota(jnp.int32, x.shape, 1)
    x = jnp.where(lane < MLA_ROPE, x, 0.0)
    ms = jnp.sum(x * x, axis=-1, keepdims=True) * (1.0 / MLA_ROPE)
    xn = x * lax.rsqrt(ms + NORM_EPS) * g
    half = MLA_ROPE // 2
    partner = jnp.where(lane < half, pltpu.roll(xn, LANES - half, axis=1), pltpu.roll(xn, half, axis=1))
    return xn * cos_ref[...] + partner * sin_ref[...]


def _q_up_kernel(x_ref, g_ref, w_ref, qn_ref, qr_ref, cos_ref, sin_ref, o_ref, h_ref, acc_ref,
                 *, heads_per_tile, n_col, n_tiles):
    s = pl.program_id(0)

    @pl.when(jnp.logical_and(s % n_col == 0, s < n_tiles))
    def _():
        x = x_ref[...]
        ms = jnp.mean(x * x, axis=-1, keepdims=True)
        h_ref[...] = (x * lax.rsqrt(ms + NORM_EPS) * g_ref[...]).astype(BF16)

    @pl.when(s == 0)
    def _():
        acc_ref[1] = jnp.zeros(acc_ref.shape[1:], acc_ref.dtype)

    scale = MLA_QK ** -0.5 * LOG2_E

    def step(fill, drain):
        for hh in range(heads_per_tile):
            cols = slice(hh * QK_PAD, (hh + 1) * QK_PAD)
            acc_ref[fill, :, cols] = _dot(h_ref[...], w_ref[:, cols])
        for hh in range(heads_per_tile):
            nope = acc_ref[drain, :, hh * QK_PAD: hh * QK_PAD + MLA_NOPE]
            ms = jnp.mean(nope * nope, axis=-1, keepdims=True)
            nope = nope * lax.rsqrt(ms + NORM_EPS) * qn_ref[...]
            pe = _rope128(acc_ref[drain, :, hh * QK_PAD + MLA_NOPE: (hh + 1) * QK_PAD], qr_ref[...], cos_ref, sin_ref)
            o_ref[hh * QK_PAD: hh * QK_PAD + MLA_NOPE, :] = (nope * scale).astype(o_ref.dtype).T
            o_ref[hh * QK_PAD + MLA_NOPE: (hh + 1) * QK_PAD, :] = (pe * scale).astype(o_ref.dtype).T

    @pl.when(s % 2 == 0)
    def _():
        step(0, 1)

    @pl.when(s % 2 == 1)
    def _():
        step(1, 0)


def _q_up(u, g, w, layer, qn_g, qr_g, cos_t, sin_t, tm, heads_per_tile):
    S = u.shape[0]
    tn = heads_per_tile * QK_PAD
    N = MLA_HEADS * QK_PAD
    n_col = N // tn
    n_tiles = (S // tm) * n_col
    fill_row = lambda s: jnp.minimum(s, n_tiles - 1) // n_col
    fill_col = lambda s: jnp.minimum(s, n_tiles - 1) % n_col
    drain_row = lambda s: jnp.maximum(s - 1, 0) // n_col
    drain_col = lambda s: jnp.maximum(s - 1, 0) % n_col
    return pl.pallas_call(
        functools.partial(_q_up_kernel, heads_per_tile=heads_per_tile, n_col=n_col, n_tiles=n_tiles),
        grid=(n_tiles + 1,),
        in_specs=[
            pl.BlockSpec((tm, Q_LORA), lambda s: (fill_row(s), OFF_QLAT // Q_LORA)),
            pl.BlockSpec((1, Q_LORA), lambda s: (0, 0)),
            pl.BlockSpec((None, Q_LORA, tn), lambda s: (layer, 0, fill_col(s))),
            pl.BlockSpec((1, LANES), lambda s: (0, 0)),
            pl.BlockSpec((1, LANES), lambda s: (0, 0)),
            pl.BlockSpec((tm, LANES), lambda s: (drain_row(s), 0)),
            pl.BlockSpec((tm, LANES), lambda s: (drain_row(s), 0)),
        ],
        out_specs=pl.BlockSpec((tn, tm), lambda s: (drain_col(s), drain_row(s))),
        out_shape=jax.ShapeDtypeStruct((N, S), BF16),
        scratch_shapes=[pltpu.VMEM((tm, Q_LORA), BF16), pltpu.VMEM((2, tm, tn), F32)],
        compiler_params=_cparams(("arbitrary",)),
        name="q_up",
    )(u, g, w, qn_g, qr_g, cos_t, sin_t)


def _kv_up_kernel(x_ref, misc_ref, g_ref, w_ref, kn_ref, kr_ref, cos_ref, sin_ref, k_ref, vt_ref):
    x = x_ref[...]
    ms = jnp.mean(x * x, axis=-1, keepdims=True)
    h = (x * lax.rsqrt(ms + NORM_EPS) * g_ref[...]).astype(BF16)
    acc = _dot(h, w_ref[...])
    pe = _rope128(misc_ref[...], kr_ref[...], cos_ref, sin_ref).astype(k_ref.dtype)
    for hd in range(MLA_HEADS):
        nope = acc[:, hd * MLA_NOPE: (hd + 1) * MLA_NOPE]
        ms = jnp.mean(nope * nope, axis=-1, keepdims=True)
        k_ref[:, hd * QK_PAD: hd * QK_PAD + MLA_NOPE] = (nope * lax.rsqrt(ms + NORM_EPS) * kn_ref[...]).astype(k_ref.dtype)
        k_ref[:, hd * QK_PAD + MLA_NOPE: (hd + 1) * QK_PAD] = pe
        v = acc[:, MLA_HEADS * MLA_NOPE + hd * MLA_V: MLA_HEADS * MLA_NOPE + (hd + 1) * MLA_V]
        vt_ref[hd * V_AUG: hd * V_AUG + MLA_V, :] = v.astype(vt_ref.dtype).T
        vt_ref[hd * V_AUG + MLA_V: (hd + 1) * V_AUG, :] = jnp.ones((V_ONES, vt_ref.shape[1]), vt_ref.dtype)


def _kv_up(u, g, w, kn_g, kr_g, cos_t, sin_t, tk):
    S = u.shape[0]
    return pl.pallas_call(
        _kv_up_kernel,
        grid=(S // tk,),
        in_specs=[
            pl.BlockSpec((tk, KV_LORA), lambda i: (i, OFF_KVLAT // KV_LORA)),
            pl.BlockSpec((tk, LANES), lambda i: (i, OFF_MISC // LANES)),
            pl.BlockSpec((1, KV_LORA), lambda i: (0, 0)),
            pl.BlockSpec((KV_LORA, 2 * D_MLA), lambda i: (0, 0)),
            pl.BlockSpec((1, LANES), lambda i: (0, 0)),
            pl.BlockSpec((1, LANES), lambda i: (0, 0)),
            pl.BlockSpec((tk, LANES), lambda i: (i, 0)),
            pl.BlockSpec((tk, LANES), lambda i: (i, 0)),
        ],
        out_specs=[pl.BlockSpec((tk, MLA_HEADS * QK_PAD), lambda i: (i, 0)),
                   pl.BlockSpec((None, MLA_HEADS * V_AUG, tk), lambda i: (i, 0, 0))],
        out_shape=[jax.ShapeDtypeStruct((S, MLA_HEADS * QK_PAD), BF16),
                   jax.ShapeDtypeStruct((S // tk, MLA_HEADS * V_AUG, tk), BF16)],
        compiler_params=_cparams(("parallel",)),
        name="kv_up",
    )(u, u, g, w, kn_g, kr_g, cos_t, sin_t)


def _attn_kernel(qt_ref, qnext_ref, k_ref, vt_ref, z_ref, o_ref, s_ref, acc_ref, cmax_ref, *, group_size):
    n_chunks, _, tk = vt_ref.shape
    tq = qt_ref.shape[1]
    n_slots = s_ref.shape[0]

    def scores(q_ref, j, slot):
        k = k_ref[pl.ds(pl.multiple_of(j * tk, tk), tk), :]
        st = _dot(k, q_ref[...])
        s_ref[slot] = st
        return jnp.max(st, axis=0, keepdims=True)

    def accumulate(j, slot, m, cmax):
        m_new = jnp.maximum(m, cmax)
        alpha = jnp.exp2(m - m_new)
        p = jnp.exp2(s_ref[slot] - m_new).astype(BF16)
        acc_ref[...] = alpha * acc_ref[...] + _dot(vt_ref[j], p)
        return m_new

    def group(j0, m, cmax, last):
        for u in range(group_size):
            if last and u == group_size - 1:
                cnext = scores(qnext_ref, 0, 0)
            else:
                cnext = scores(qt_ref, j0 + u + 1, (u + 1) % n_slots)
            m = accumulate(j0 + u, u % n_slots, m, cmax)
            cmax = cnext
        return m, cmax

    @pl.when(pl.program_id(1) == 0)
    def _():
        cmax_ref[...] = scores(qt_ref, 0, 0)

    acc_ref[...] = jnp.zeros_like(acc_ref)
    m = jnp.full((1, tq), NEG_BIG, F32)
    n_groups = n_chunks // group_size
    m, cmax = lax.fori_loop(0, n_groups - 1, lambda g, c: group(g * group_size, c[0], c[1], False),
                            (m, cmax_ref[...]))
    _, cmax_ref[...] = group((n_groups - 1) * group_size, m, cmax, True)
    o = (acc_ref[0:MLA_V, :] / acc_ref[MLA_V:MLA_V + 1, :]).T
    o_ref[...] = (o * _silu(z_ref[...])).astype(o_ref.dtype)


def _attn(qt, kf, vt, u, tq, group_size, n_slots):
    S = kf.shape[0]
    n_chunks, _, tk = vt.shape
    n_q = S // tq
    assert n_chunks % group_size == 0 and group_size % n_slots == 0 and n_slots >= 2
    return pl.pallas_call(
        functools.partial(_attn_kernel, group_size=group_size),
        grid=(MLA_HEADS, n_q),
        in_specs=[
            pl.BlockSpec((QK_PAD, tq), lambda h, i: (h, i)),
            pl.BlockSpec((QK_PAD, tq), lambda h, i: (h, jnp.minimum(i + 1, n_q - 1))),
            pl.BlockSpec((S, QK_PAD), lambda h, i: (0, h)),
            pl.BlockSpec((n_chunks, V_AUG, tk), lambda h, i: (0, h, 0)),
            pl.BlockSpec((tq, MLA_V), lambda h, i: (i, OFF_AZ // MLA_V + h)),
        ],
        out_specs=pl.BlockSpec((tq, MLA_V), lambda h, i: (i, h)),
        out_shape=jax.ShapeDtypeStruct((S, D_MLA), BF16),
        scratch_shapes=[pltpu.VMEM((n_slots, tk, tq), F32), pltpu.VMEM((V_AUG, tq), F32),
                        pltpu.VMEM((1, tq), F32)],
        compiler_params=_cparams(("parallel", "arbitrary")),
        name="attn",
    )(qt, qt, kf, vt, u)


def _out_proj_kernel(p_ref, m_ref, a_ref, w_ref, x_ref, o_ref):
    acc = _dot(p_ref[...], w_ref[0:D_POOL, :])
    acc += _dot(m_ref[...], w_ref[D_POOL:D_POOL + D_MLSTM, :])
    acc += _dot(a_ref[...], w_ref[D_POOL + D_MLSTM:, :])
    o_ref[...] = x_ref[...] + acc


def _out_proj(pool_o, mlstm_o, mla_o, w, layer, x, tm, tn):
    S = x.shape[0]
    return pl.pallas_call(
        _out_proj_kernel,
        grid=(S // tm, D_MODEL // tn),
        in_specs=[
            pl.BlockSpec((tm, D_POOL), lambda i, j: (i, 0)),
            pl.BlockSpec((tm, D_MLSTM), lambda i, j: (i, 0)),
            pl.BlockSpec((tm, D_MLA), lambda i, j: (i, 0)),
            pl.BlockSpec((None, D_MODEL, tn), lambda i, j: (layer, 0, j)),
            pl.BlockSpec((tm, tn), lambda i, j: (i, j)),
        ],
        out_specs=pl.BlockSpec((tm, tn), lambda i, j: (i, j)),
        out_shape=jax.ShapeDtypeStruct((S, D_MODEL), F32),
        compiler_params=_cparams(("parallel", "arbitrary")),
        name="out_proj",
    )(pool_o, mlstm_o, mla_o, w, x)


def _pad_lanes(v, offset=0):
    return jnp.zeros((1, LANES), F32).at[0, offset:offset + v.shape[0]].set(v.astype(F32))


_SRC_SIZES = (D_POOL, D_POOL, D_MLSTM, D_MLSTM, D_MLSTM, D_MLSTM, D_MLSTM, MLSTM_GATES,
              Q_LORA, KV_LORA, MLA_ROPE, D_MLA)
(_SRC_PX, _, _, _, _, _, _, _SRC_MG, _SRC_QLAT, _SRC_KVLAT, _SRC_KROPE, _SRC_AZ) = (
    int(o) for o in np.concatenate([[0], np.cumsum(_SRC_SIZES)])[:-1])
W_IN_MOVES = (
    (OFF_QLAT, _SRC_QLAT, Q_LORA), (OFF_KVLAT, _SRC_KVLAT, KV_LORA), (OFF_AZ, _SRC_AZ, D_MLA),
    (OFF_PX, _SRC_PX, 2 * D_POOL + 5 * D_MLSTM),
)
W_IN_TILE = 256
W_IN_MISC_TILE = OFF_MISC // W_IN_TILE


def _w_in_src_row(j):
    unit = 16
    row = jnp.int32(_SRC_KROPE // unit)
    for dst, src, width in reversed(W_IN_MOVES):
        row = jnp.where(j < (dst + width) // W_IN_TILE, src // unit + (j - dst // W_IN_TILE) * (W_IN_TILE // unit), row)
    return row * unit


def _w_in_prep_kernel(rows_ref, gate_rows_ref, o_ref):
    j = pl.program_id(1)

    @pl.when(j != W_IN_MISC_TILE)
    def _():
        o_ref[...] = rows_ref[0].T.astype(o_ref.dtype)

    @pl.when(j == W_IN_MISC_TILE)
    def _():
        pad = jnp.zeros((W_IN_TILE - MLA_ROPE - MLSTM_GATES, rows_ref.shape[2]), rows_ref.dtype)
        tile = jnp.concatenate([rows_ref[0, 0:MLA_ROPE, :], gate_rows_ref[0, 0:MLSTM_GATES, :], pad], axis=0)
        o_ref[...] = tile.T.astype(o_ref.dtype)


def _prep_w_in(w_in):
    wt = jnp.swapaxes(w_in, 1, 2)
    n_layers, _, K = wt.shape
    elem = pl.Element
    return pl.pallas_call(
        _w_in_prep_kernel,
        grid=(n_layers, N_IN_PAD // W_IN_TILE),
        in_specs=[pl.BlockSpec((elem(1), elem(W_IN_TILE), elem(K)), lambda l, j: (l, _w_in_src_row(j), 0)),
                  pl.BlockSpec((elem(1), elem(W_IN_TILE), elem(K)), lambda l, j: (l, _SRC_MG, 0))],
        out_specs=pl.BlockSpec((None, K, W_IN_TILE), lambda l, j: (l, 0, j)),
        out_shape=jax.ShapeDtypeStruct((n_layers, K, N_IN_PAD), BF16),
        compiler_params=_cparams(("parallel", "parallel")),
        name="w_in_prep",
    )(wt, wt)


def _w_uq_prep_kernel(w_ref, o_ref):
    for hd in range(MLA_HEADS):
        o_ref[:, hd * QK_PAD: hd * QK_PAD + MLA_QK] = w_ref[:, hd * MLA_QK: (hd + 1) * MLA_QK].astype(o_ref.dtype)
        o_ref[:, hd * QK_PAD + MLA_QK: (hd + 1) * QK_PAD] = jnp.zeros((o_ref.shape[0], QK_PAD - MLA_QK), o_ref.dtype)


def _prep_weight(body, w, n_out, tr, name):
    n_layers, K, N = w.shape
    return pl.pallas_call(
        body,
        grid=(n_layers, K // tr),
        in_specs=[pl.BlockSpec((None, tr, N), lambda l, i: (l, i, 0))],
        out_specs=pl.BlockSpec((None, tr, n_out), lambda l, i: (l, i, 0)),
        out_shape=jax.ShapeDtypeStruct((n_layers, K, n_out), BF16),
        compiler_params=_cparams(("parallel", "parallel")),
        name=name,
    )(w)


def _prep_layer(norm_g, gate_bias, pool_w, pool_scale, mlstm_norm_g, qlat_g, kvlat_g, w_ukv, qn_g, qr_g, kn_g, kr_g):
    wkv = w_ukv.reshape(KV_LORA, MLA_HEADS, MLA_NOPE + MLA_V)
    w_ukv_p = jnp.concatenate([wkv[:, :, :MLA_NOPE].reshape(KV_LORA, -1),
                               wkv[:, :, MLA_NOPE:].reshape(KV_LORA, -1)], axis=1).astype(BF16)
    return dict(
        norm_g=norm_g.reshape(1, D_MODEL),
        bias_misc=_pad_lanes(gate_bias, MISC_GATE0),
        pool_w=pool_w.astype(BF16), pool_scale=pool_scale.reshape(1, D_POOL),
        mlstm_norm_g=mlstm_norm_g.reshape(1, D_MLSTM),
        qlat_g=qlat_g.reshape(1, Q_LORA),
        kvlat_g=kvlat_g.reshape(1, KV_LORA), w_ukv=w_ukv_p,
        qn_g=qn_g.reshape(1, MLA_NOPE), qr_g=_pad_lanes(qr_g),
        kn_g=kn_g.reshape(1, MLA_NOPE), kr_g=_pad_lanes(kr_g),
    )


def _rope_tables(S):
    pos = jnp.arange(S, dtype=F32)
    inv_freq = ROPE_THETA ** (-(jnp.arange(0, MLA_ROPE, 2, dtype=F32) / MLA_ROPE))
    ang = pos[:, None] * inv_freq[None, :]
    cos, sin = jnp.cos(ang), jnp.sin(ang)
    pad = jnp.zeros((S, LANES - MLA_ROPE), F32)
    return jnp.concatenate([cos, cos, pad], axis=1), jnp.concatenate([-sin, sin, pad], axis=1)


ATTN_SLOTS = 2


def _tiles(S):
    t = lambda pref: min(pref, S)
    n_chunks = S // t(512)
    slots = min(ATTN_SLOTS, n_chunks)
    return dict(tm_in=t(512), tn_in=768, tm_pool=t(512), tm_mo=t(512), tm_q=t(512), q_heads=4,
                tk=t(512), tq=t(512), attn_group=min(16, n_chunks), attn_slots=slots, tm_out=t(512), tn_out=1024)


def _layer(x, layer, p, big, cos_t, sin_t):
    S = x.shape[0]
    t = _tiles(S)
    u = _in_proj(x, p["norm_g"], big["w_in"], layer, t["tm_in"], t["tn_in"])
    pool_o = _pool(u, p["pool_w"], p["pool_scale"], t["tm_pool"])
    hf, hb = _mlstm(u, p["bias_misc"])
    mlstm_o = _mlstm_out(hf, hb, u, p["mlstm_norm_g"], t["tm_mo"])
    qt = _q_up(u, p["qlat_g"], big["w_uq"], layer, p["qn_g"], p["qr_g"], cos_t, sin_t, t["tm_q"], t["q_heads"])
    kf, vt = _kv_up(u, p["kvlat_g"], p["w_ukv"], p["kn_g"], p["kr_g"], cos_t, sin_t, t["tk"])
    mla_o = _attn(qt, kf, vt, u, t["tq"], t["attn_group"], t["attn_slots"])
    return _out_proj(pool_o, mlstm_o, mla_o, big["w_out"], layer, x, t["tm_out"], t["tn_out"])


def _trunk(x, layers, big, cos_t, sin_t):
    x = x[0]
    for layer, p in enumerate(layers):
        x = _layer(x, layer, p, big, cos_t, sin_t)
    return x[None]


def kernel(x_prompt, x_sample, norm_g, w_in, gate_bias, pool_w, pool_scale, mlstm_norm_g,
           qlat_g, w_uq, kvlat_g, w_ukv, qn_g, qr_g, kn_g, kr_g, w_out):
    small = (norm_g, gate_bias, pool_w, pool_scale, mlstm_norm_g, qlat_g, kvlat_g, w_ukv, qn_g, qr_g, kn_g, kr_g)
    layers = [_prep_layer(*[w[l] for w in small]) for l in range(norm_g.shape[0])]
    big = dict(w_in=_prep_w_in(w_in),
               w_uq=_prep_weight(_w_uq_prep_kernel, w_uq, MLA_HEADS * QK_PAD, 512, "w_uq_prep"),
               w_out=w_out.astype(BF16))
    cos_t, sin_t = _rope_tables(max(x_prompt.shape[1], x_sample.shape[1]))
    return (_trunk(x_prompt, layers, big, cos_t, sin_t), _trunk(x_sample, layers, big, cos_t, sin_t))
```

```python
import functools

import numpy as np
import jax
import jax.numpy as jnp
from jax import lax
from jax.experimental import pallas as pl
from jax.experimental.pallas import tpu as pltpu

D_MODEL = 4096
D_POOL = 1024
D_MLSTM = 1024
D_MLA = 2048
POOL_WINDOWS = (2, 4, 8, 16)
POOL_GW = 256
MLSTM_HEADS = 4
MLSTM_HD = 256
MLSTM_CHUNK = 128
MLSTM_GATES = 16
MLA_HEADS = 16
MLA_V = 128
MLA_NOPE = 128
MLA_ROPE = 64
MLA_QK = 192
Q_LORA = 1536
KV_LORA = 512
ROPE_THETA = 10000.0
NORM_EPS = 1e-6

LANES = 128
QK_PAD = 256
V_ONES = 16
V_AUG = MLA_V + V_ONES
LOG2_E = 1.4426950408889634
VMEM_LIMIT = 56 * 1024 * 1024

OFF_QLAT = 0
OFF_KVLAT = 1536
OFF_AZ = 2048
OFF_PX = 4096
OFF_PZ = 5120
OFF_MQ = 6144
OFF_MK = 7168
OFF_MV = 8192
OFF_MO = 9216
OFF_MZ = 10240
OFF_MISC = 11264
N_IN_PAD = 11520
MISC_GATE0 = MLA_ROPE

NEG_BIG = -1e30

F32 = jnp.float32
BF16 = jnp.bfloat16


def _cparams(sem):
    return pltpu.CompilerParams(dimension_semantics=sem, vmem_limit_bytes=VMEM_LIMIT)


def _sigmoid(x):
    return 1.0 / (1.0 + jnp.exp(-x))


def _silu(x):
    return x * _sigmoid(x)


def _dot(a, b):
    return jnp.dot(a, b, preferred_element_type=F32)


def _dot_nt(a, b):
    return lax.dot_general(a, b, (((1,), (1,)), ((), ())), preferred_element_type=F32)


def _dot_tn(a, b):
    return lax.dot_general(a, b, (((0,), (0,)), ((), ())), preferred_element_type=F32)


def _norm_matmul_kernel(x_ref, g_ref, w_ref, o_ref, h_ref):
    i, j = pl.program_id(0), pl.program_id(1)
    last_col = j == pl.num_programs(1) - 1

    def normalise(slot):
        x = x_ref[...]
        ms = jnp.mean(x * x, axis=-1, keepdims=True)
        h_ref[slot] = (x * lax.rsqrt(ms + NORM_EPS) * g_ref[...]).astype(BF16)

    @pl.when(jnp.logical_and(i == 0, j == 0))
    def _():
        normalise(0)

    for slot in (0, 1):
        @pl.when(jnp.logical_and(i % 2 == slot, jnp.logical_not(last_col)))
        def _():
            o_ref[...] = _dot(h_ref[slot], w_ref[...]).astype(o_ref.dtype)

        @pl.when(jnp.logical_and(i % 2 == slot, last_col))
        def _():
            o_ref[...] = _dot(h_ref[slot], w_ref[...]).astype(o_ref.dtype)
            normalise(1 - slot)


def _in_proj(x, g, w, layer, tm, tn):
    S, K = x.shape
    N = w.shape[2]
    n_row, n_col = S // tm, N // tn
    x_row = lambda i, j: jnp.minimum(i + (j == n_col - 1).astype(jnp.int32), n_row - 1)
    return pl.pallas_call(
        _norm_matmul_kernel,
        grid=(n_row, n_col),
        in_specs=[
            pl.BlockSpec((tm, K), lambda i, j: (x_row(i, j), 0)),
            pl.BlockSpec((1, K), lambda i, j: (0, 0)),
            pl.BlockSpec((None, K, tn), lambda i, j: (layer, 0, j)),
        ],
        out_specs=pl.BlockSpec((tm, tn), lambda i, j: (i, j)),
        out_shape=jax.ShapeDtypeStruct((S, N), F32),
        scratch_shapes=[pltpu.VMEM((2, tm, K), BF16)],
        compiler_params=_cparams(("arbitrary", "arbitrary")),
        name="in_proj",
    )(x, g, w)


POOL_HALO = 8


def _pool_kernel(prev_ref, cur_ref, next_ref, z_ref, w_ref, scale_ref, o_ref, *, seq_len):
    i = pl.program_id(0)
    tm = cur_ref.shape[0]
    rows = tm + 2 * POOL_HALO
    prev = jnp.where(i > 0, prev_ref[...], 0.0)
    nxt = jnp.where(i < pl.num_programs(0) - 1, next_ref[...], 0.0)
    cur = cur_ref[...]
    xall = jnp.concatenate([prev, cur, nxt], axis=0)
    pos = i * tm + lax.broadcasted_iota(jnp.int32, (tm, 1), 0)
    for g, win in enumerate(POOL_WINDOWS):
        half = win // 2
        sl = slice(g * POOL_GW, (g + 1) * POOL_GW)
        acc = xall[:, sl]
        span = 1
        while span < win:
            acc = acc + pltpu.roll(acc, span, axis=0)
            span *= 2
        shift = half - 1
        if shift:
            acc = pltpu.roll(acc, rows - shift, axis=0)
        wsum = acc[POOL_HALO:POOL_HALO + tm]
        cnt = jnp.minimum(pos + half, seq_len) - jnp.maximum(pos - half, 0)
        diff = wsum / cnt.astype(F32) - cur[:, sl]
        y = _dot(diff.astype(BF16), w_ref[g])
        o_ref[:, sl] = (y * scale_ref[:, sl] * _silu(z_ref[:, sl])).astype(o_ref.dtype)


def _pool(u, pool_w, pool_scale, tm):
    S = u.shape[0]
    hb = tm // POOL_HALO
    n_halo = S // POOL_HALO
    cpx = OFF_PX // D_POOL
    return pl.pallas_call(
        functools.partial(_pool_kernel, seq_len=S),
        grid=(S // tm,),
        in_specs=[
            pl.BlockSpec((POOL_HALO, D_POOL), lambda i: (jnp.maximum(i * hb - 1, 0), cpx)),
            pl.BlockSpec((tm, D_POOL), lambda i: (i, cpx)),
            pl.BlockSpec((POOL_HALO, D_POOL), lambda i: (jnp.minimum((i + 1) * hb, n_halo - 1), cpx)),
            pl.BlockSpec((tm, D_POOL), lambda i: (i, OFF_PZ // D_POOL)),
            pl.BlockSpec((len(POOL_WINDOWS), POOL_GW, POOL_GW), lambda i: (0, 0, 0)),
            pl.BlockSpec((1, D_POOL), lambda i: (0, 0)),
        ],
        out_specs=pl.BlockSpec((tm, D_POOL), lambda i: (i, 0)),
        out_shape=jax.ShapeDtypeStruct((S, D_POOL), BF16),
        compiler_params=_cparams(("parallel",)),
        name="pool",
    )(u, u, u, u, pool_w, pool_scale)


def _split3(x):
    hi = x.astype(BF16)
    r1 = x - hi.astype(F32)
    mid = r1.astype(BF16)
    lo = (r1 - mid.astype(F32)).astype(BF16)
    return hi, mid, lo


def _log_sigmoid(x):
    return jnp.minimum(x, 0.0) - jnp.log(1.0 + jnp.exp(-jnp.abs(x)))


def _lane_tile(x, reps):
    return jnp.concatenate([x] * reps, axis=1)


def _mlstm_chain(q, k, v, i_rep, b_rep, pm_rep, i_row, b_row, c_old, m, reverse):
    L = q.shape[0]
    row = lax.broadcasted_iota(jnp.int32, (L, L), 0)
    col = lax.broadcasted_iota(jnp.int32, (L, L), 1)
    keep = (col >= row) if reverse else (col <= row)
    b_last = b_rep[0:1] if reverse else b_rep[L - 1:L]

    m_t = b_rep + jnp.maximum(m, pm_rep)
    qb = q.astype(BF16)
    ks = k * (MLSTM_HD ** -0.5)
    v_aug = jnp.concatenate([v.astype(BF16), jnp.ones((L, LANES), BF16)], axis=1)
    dexp = jnp.exp(jnp.where(keep, (b_rep - m_t) - (b_row - i_row), NEG_BIG))
    w = dexp * _dot_nt(qb, ks.astype(BF16))
    ei = jnp.exp(b_rep + m - m_t)
    numden = (_lane_tile(ei, c_old.shape[1] // LANES) * _dot(qb, c_old.astype(BF16))
              + _dot(w.astype(BF16), v_aug))
    den = jnp.maximum(jnp.abs(numden[:, MLSTM_HD:]), jnp.exp(-m_t))
    h = numden[:, :MLSTM_HD] / _lane_tile(den, MLSTM_HD // LANES)

    a_rep = b_last - b_rep + i_rep
    m_new = jnp.maximum(b_last + m, jnp.max(a_rep, axis=0, keepdims=True))
    decay = jnp.exp(b_last + m - m_new)
    ek = _lane_tile(jnp.exp(a_rep - m_new), MLSTM_HD // LANES) * ks
    c_new = _lane_tile(decay, c_old.shape[1] // LANES) * c_old + _dot_tn(ek.astype(BF16), v_aug)
    return h, c_new, m_new


def _mlstm_kernel(qf_ref, kf_ref, vf_ref, gf_ref, qb_ref, kb_ref, vb_ref, gb_ref, bias_ref,
                  hf_ref, hb_ref, c_ref, m_ref):
    @pl.when(pl.program_id(0) == 0)
    def _():
        c_ref[...] = jnp.zeros_like(c_ref)
        m_ref[...] = jnp.zeros_like(m_ref)

    L = gf_ref.shape[0]
    row = lax.broadcasted_iota(jnp.int32, (L, L), 0)
    col = lax.broadcasted_iota(jnp.int32, (L, L), 1)
    dirs = ((qf_ref, kf_ref, vf_ref, gf_ref, hf_ref, False), (qb_ref, kb_ref, vb_ref, gb_ref, hb_ref, True))
    gates = []
    for d, (q_ref, k_ref, v_ref, g_ref, h_ref, reverse) in enumerate(dirs):
        g = g_ref[...] + bias_ref[...]
        lane_i0 = MISC_GATE0 + (2 * d) * MLSTM_HEADS
        tri = jnp.where((col >= row) if reverse else (col <= row), 1.0, 0.0).astype(BF16)
        b_all = sum(_dot(tri, p) for p in _split3(_log_sigmoid(g)))
        pm_all = g - pltpu.roll(b_all, LANES - MLSTM_HEADS, axis=1)
        shift = 1
        while shift < L:
            if reverse:
                moved = jnp.where(row < L - shift, pltpu.roll(pm_all, L - shift, axis=0), NEG_BIG)
            else:
                moved = jnp.where(row >= shift, pltpu.roll(pm_all, shift, axis=0), NEG_BIG)
            pm_all = jnp.maximum(pm_all, moved)
            shift *= 2
        gt = g.T
        bt = b_all.T
        for hd in range(MLSTM_HEADS):
            lane_i = lane_i0 + hd
            lane_f = lane_i0 + MLSTM_HEADS + hd
            lane_rep = lambda x, lane: jnp.broadcast_to(x[:, lane:lane + 1], (L, LANES))
            gates.append((lane_rep(g, lane_i), lane_rep(b_all, lane_f), lane_rep(pm_all, lane_i),
                          gt[lane_i:lane_i + 1, :], bt[lane_f:lane_f + 1, :]))
    results = []
    for d, (q_ref, k_ref, v_ref, g_ref, h_ref, reverse) in enumerate(dirs):
        for hd in range(MLSTM_HEADS):
            sl = slice(hd * MLSTM_HD, (hd + 1) * MLSTM_HD)
            idx = d * MLSTM_HEADS + hd
            h, c_new, m_new = _mlstm_chain(q_ref[:, sl], k_ref[:, sl], v_ref[:, sl], *gates[idx],
                                           c_ref[idx], m_ref[idx], reverse)
            results.append((h_ref, sl, idx, h, c_new, m_new))
    for h_ref, sl, idx, h, c_new, m_new in results:
        h_ref[:, sl] = h
        c_ref[idx] = c_new
        m_ref[idx] = m_new


def _mlstm(u, bias_misc):
    S = u.shape[0]
    L = MLSTM_CHUNK
    nc = S // L
    fwd = lambda c: c
    bwd = lambda c: nc - 1 - c

    def spec(off, width, cmap):
        return pl.BlockSpec((L, width), lambda c: (cmap(c), off // width))

    in_specs = []
    for cmap in (fwd, bwd):
        in_specs += [spec(OFF_MQ, D_MLSTM, cmap), spec(OFF_MK, D_MLSTM, cmap), spec(OFF_MV, D_MLSTM, cmap),
                     spec(OFF_MISC, LANES, cmap)]
    in_specs.append(pl.BlockSpec((1, LANES), lambda c: (0, 0)))
    nstate = 2 * MLSTM_HEADS
    return pl.pallas_call(
        _mlstm_kernel,
        grid=(nc,),
        in_specs=in_specs,
        out_specs=[pl.BlockSpec((L, D_MLSTM), lambda c: (c, 0)),
                   pl.BlockSpec((L, D_MLSTM), lambda c: (nc - 1 - c, 0))],
        out_shape=[jax.ShapeDtypeStruct((S, D_MLSTM), F32)] * 2,
        scratch_shapes=[pltpu.VMEM((nstate, MLSTM_HD, MLSTM_HD + LANES), F32),
                        pltpu.VMEM((nstate, 1, LANES), F32)],
        compiler_params=_cparams(("arbitrary",)),
        name="mlstm",
    )(u, u, u, u, u, u, u, u, bias_misc)


def _mlstm_out_kernel(hf_ref, hb_ref, o_ref, z_ref, g_ref, out_ref):
    for hd in range(MLSTM_HEADS):
        sl = slice(hd * MLSTM_HD, (hd + 1) * MLSTM_HD)
        h = hf_ref[:, sl] + hb_ref[:, sl]
        ms = jnp.mean(h * h, axis=-1, keepdims=True)
        hn = h * lax.rsqrt(ms + NORM_EPS) * g_ref[:, sl]
        out_ref[:, sl] = (_sigmoid(o_ref[:, sl]) * hn * _silu(z_ref[:, sl])).astype(out_ref.dtype)


def _mlstm_out(hf, hb, u, norm_g, tm):
    S = hf.shape[0]
    return pl.pallas_call(
        _mlstm_out_kernel,
        grid=(S // tm,),
        in_specs=[
            pl.BlockSpec((tm, D_MLSTM), lambda i: (i, 0)),
            pl.BlockSpec((tm, D_MLSTM), lambda i: (i, 0)),
            pl.BlockSpec((tm, D_MLSTM), lambda i: (i, OFF_MO // D_MLSTM)),
            pl.BlockSpec((tm, D_MLSTM), lambda i: (i, OFF_MZ // D_MLSTM)),
            pl.BlockSpec((1, D_MLSTM), lambda i: (0, 0)),
        ],
        out_specs=pl.BlockSpec((tm, D_MLSTM), lambda i: (i, 0)),
        out_shape=jax.ShapeDtypeStruct((S, D_MLSTM), BF16),
        compiler_params=_cparams(("parallel",)),
        name="mlstm_out",
    )(hf, hb, u, u, norm_g)


def _rope128(x, g, cos_ref, sin_ref):
    lane = lax.broadcasted_iota(jnp.int32, x.shape, 1)
    x = jnp.where(lane < MLA_ROPE, x, 0.0)
    ms = jnp.sum(x * x, axis=-1, keepdims=True) * (1.0 / MLA_ROPE)
    xn = x * lax.rsqrt(ms + NORM_EPS) * g
    half = MLA_ROPE // 2
    partner = jnp.where(lane < half, pltpu.roll(xn, LANES - half, axis=1), pltpu.roll(xn, half, axis=1))
    return xn * cos_ref[...] + partner * sin_ref[...]


def _q_up_kernel(x_ref, g_ref, w_ref, qn_ref, qr_ref, cos_ref, sin_ref, o_ref, h_ref, acc_ref,
                 *, heads_per_tile, n_col, n_tiles):
    s = pl.program_id(0)

    @pl.when(jnp.logical_and(s % n_col == 0, s < n_tiles))
    def _():
        x = x_ref[...]
        ms = jnp.mean(x * x, axis=-1, keepdims=True)
        h_ref[...] = (x * lax.rsqrt(ms + NORM_EPS) * g_ref[...]).astype(BF16)

    @pl.when(s == 0)
    def _():
        acc_ref[1] = jnp.zeros(acc_ref.shape[1:], acc_ref.dtype)

    scale = MLA_QK ** -0.5 * LOG2_E

    def step(fill, drain):
        for hh in range(heads_per_tile):
            cols = slice(hh * QK_PAD, (hh + 1) * QK_PAD)
            acc_ref[fill, :, cols] = _dot(h_ref[...], w_ref[:, cols])
        for hh in range(heads_per_tile):
            nope = acc_ref[drain, :, hh * QK_PAD: hh * QK_PAD + MLA_NOPE]
            ms = jnp.mean(nope * nope, axis=-1, keepdims=True)
            nope = nope * lax.rsqrt(ms + NORM_EPS) * qn_ref[...]
            pe = _rope128(acc_ref[drain, :, hh * QK_PAD + MLA_NOPE: (hh + 1) * QK_PAD], qr_ref[...], cos_ref, sin_ref)
            o_ref[hh * QK_PAD: hh * QK_PAD + MLA_NOPE, :] = (nope * scale).astype(o_ref.dtype).T
            o_ref[hh * QK_PAD + MLA_NOPE: (hh + 1) * QK_PAD, :] = (pe * scale).astype(o_ref.dtype).T

    @pl.when(s % 2 == 0)
    def _():
        step(0, 1)

    @pl.when(s % 2 == 1)
    def _():
        step(1, 0)


def _q_up(u, g, w, layer, qn_g, qr_g, cos_t, sin_t, tm, heads_per_tile):
    S = u.shape[0]
    tn = heads_per_tile * QK_PAD
    N = MLA_HEADS * QK_PAD
    n_col = N // tn
    n_tiles = (S // tm) * n_col
    fill_row = lambda s: jnp.minimum(s, n_tiles - 1) // n_col
    fill_col = lambda s: jnp.minimum(s, n_tiles - 1) % n_col
    drain_row = lambda s: jnp.maximum(s - 1, 0) // n_col
    drain_col = lambda s: jnp.maximum(s - 1, 0) % n_col
    return pl.pallas_call(
        functools.partial(_q_up_kernel, heads_per_tile=heads_per_tile, n_col=n_col, n_tiles=n_tiles),
        grid=(n_tiles + 1,),
        in_specs=[
            pl.BlockSpec((tm, Q_LORA), lambda s: (fill_row(s), OFF_QLAT // Q_LORA)),
            pl.BlockSpec((1, Q_LORA), lambda s: (0, 0)),
            pl.BlockSpec((None, Q_LORA, tn), lambda s: (layer, 0, fill_col(s))),
            pl.BlockSpec((1, LANES), lambda s: (0, 0)),
            pl.BlockSpec((1, LANES), lambda s: (0, 0)),
            pl.BlockSpec((tm, LANES), lambda s: (drain_row(s), 0)),
            pl.BlockSpec((tm, LANES), lambda s: (drain_row(s), 0)),
        ],
        out_specs=pl.BlockSpec((tn, tm), lambda s: (drain_col(s), drain_row(s))),
        out_shape=jax.ShapeDtypeStruct((N, S), BF16),
        scratch_shapes=[pltpu.VMEM((tm, Q_LORA), BF16), pltpu.VMEM((2, tm, tn), F32)],
        compiler_params=_cparams(("arbitrary",)),
        name="q_up",
    )(u, g, w, qn_g, qr_g, cos_t, sin_t)


def _kv_up_kernel(x_ref, misc_ref, g_ref, w_ref, kn_ref, kr_ref, cos_ref, sin_ref, k_ref, vt_ref):
    x = x_ref[...]
    ms = jnp.mean(x * x, axis=-1, keepdims=True)
    h = (x * lax.rsqrt(ms + NORM_EPS) * g_ref[...]).astype(BF16)
    acc = _dot(h, w_ref[...])
    pe = _rope128(misc_ref[...], kr_ref[...], cos_ref, sin_ref).astype(k_ref.dtype)
    for hd in range(MLA_HEADS):
        nope = acc[:, hd * MLA_NOPE: (hd + 1) * MLA_NOPE]
        ms = jnp.mean(nope * nope, axis=-1, keepdims=True)
        k_ref[:, hd * QK_PAD: hd * QK_PAD + MLA_NOPE] = (nope * lax.rsqrt(ms + NORM_EPS) * kn_ref[...]).astype(k_ref.dtype)
        k_ref[:, hd * QK_PAD + MLA_NOPE: (hd + 1) * QK_PAD] = pe
        v = acc[:, MLA_HEADS * MLA_NOPE + hd * MLA_V: MLA_HEADS * MLA_NOPE + (hd + 1) * MLA_V]
        vt_ref[hd * V_AUG: hd * V_AUG + MLA_V, :] = v.astype(vt_ref.dtype).T
        vt_ref[hd * V_AUG + MLA_V: (hd + 1) * V_AUG, :] = jnp.ones((V_ONES, vt_ref.shape[1]), vt_ref.dtype)


def _kv_up(u, g, w, kn_g, kr_g, cos_t, sin_t, tk):
    S = u.shape[0]
    return pl.pallas_call(
        _kv_up_kernel,
        grid=(S // tk,),
        in_specs=[
            pl.BlockSpec((tk, KV_LORA), lambda i: (i, OFF_KVLAT // KV_LORA)),
            pl.BlockSpec((tk, LANES), lambda i: (i, OFF_MISC // LANES)),
            pl.BlockSpec((1, KV_LORA), lambda i: (0, 0)),
            pl.BlockSpec((KV_LORA, 2 * D_MLA), lambda i: (0, 0)),
            pl.BlockSpec((1, LANES), lambda i: (0, 0)),
            pl.BlockSpec((1, LANES), lambda i: (0, 0)),
            pl.BlockSpec((tk, LANES), lambda i: (i, 0)),
            pl.BlockSpec((tk, LANES), lambda i: (i, 0)),
        ],
        out_specs=[pl.BlockSpec((tk, MLA_HEADS * QK_PAD), lambda i: (i, 0)),
                   pl.BlockSpec((None, MLA_HEADS * V_AUG, tk), lambda i: (i, 0, 0))],
        out_shape=[jax.ShapeDtypeStruct((S, MLA_HEADS * QK_PAD), BF16),
                   jax.ShapeDtypeStruct((S // tk, MLA_HEADS * V_AUG, tk), BF16)],
        compiler_params=_cparams(("parallel",)),
        name="kv_up",
    )(u, u, g, w, kn_g, kr_g, cos_t, sin_t)


def _attn_kernel(qt_ref, qnext_ref, k_ref, vt_ref, z_ref, o_ref, s_ref, acc_ref, cmax_ref, *, group_size):
    n_chunks, _, tk = vt_ref.shape
    tq = qt_ref.shape[1]
    n_slots = s_ref.shape[0]

    def scores(q_ref, j, slot):
        k = k_ref[pl.ds(pl.multiple_of(j * tk, tk), tk), :]
        st = _dot(k, q_ref[...])
        s_ref[slot] = st
        return jnp.max(st, axis=0, keepdims=True)

    def accumulate(j, slot, m, cmax):
        m_new = jnp.maximum(m, cmax)
        alpha = jnp.exp2(m - m_new)
        p = jnp.exp2(s_ref[slot] - m_new).astype(BF16)
        acc_ref[...] = alpha * acc_ref[...] + _dot(vt_ref[j], p)
        return m_new

    def group(j0, m, cmax, last):
        for u in range(group_size):
            if last and u == group_size - 1:
                cnext = scores(qnext_ref, 0, 0)
            else:
                cnext = scores(qt_ref, j0 + u + 1, (u + 1) % n_slots)
            m = accumulate(j0 + u, u % n_slots, m, cmax)
            cmax = cnext
        return m, cmax

    @pl.when(pl.program_id(1) == 0)
    def _():
        cmax_ref[...] = scores(qt_ref, 0, 0)

    acc_ref[...] = jnp.zeros_like(acc_ref)
    m = jnp.full((1, tq), NEG_BIG, F32)
    n_groups = n_chunks // group_size
    m, cmax = lax.fori_loop(0, n_groups - 1, lambda g, c: group(g * group_size, c[0], c[1], False),
                            (m, cmax_ref[...]))
    _, cmax_ref[...] = group((n_groups - 1) * group_size, m, cmax, True)
    o = (acc_ref[0:MLA_V, :] / acc_ref[MLA_V:MLA_V + 1, :]).T
    o_ref[...] = (o * _silu(z_ref[...])).astype(o_ref.dtype)


def _attn(qt, kf, vt, u, tq, group_size, n_slots):
    S = kf.shape[0]
    n_chunks, _, tk = vt.shape
    n_q = S // tq
    assert n_chunks % group_size == 0 and group_size % n_slots == 0 and n_slots >= 2
    return pl.pallas_call(
        functools.partial(_attn_kernel, group_size=group_size),
        grid=(MLA_HEADS, n_q),
        in_specs=[
            pl.BlockSpec((QK_PAD, tq), lambda h, i: (h, i)),
            pl.BlockSpec((QK_PAD, tq), lambda h, i: (h, jnp.minimum(i + 1, n_q - 1))),
            pl.BlockSpec((S, QK_PAD), lambda h, i: (0, h)),
            pl.BlockSpec((n_chunks, V_AUG, tk), lambda h, i: (0, h, 0)),
            pl.BlockSpec((tq, MLA_V), lambda h, i: (i, OFF_AZ // MLA_V + h)),
        ],
        out_specs=pl.BlockSpec((tq, MLA_V), lambda h, i: (i, h)),
        out_shape=jax.ShapeDtypeStruct((S, D_MLA), BF16),
        scratch_shapes=[pltpu.VMEM((n_slots, tk, tq), F32), pltpu.VMEM((V_AUG, tq), F32),
                        pltpu.VMEM((1, tq), F32)],
        compiler_params=_cparams(("parallel", "arbitrary")),
        name="attn",
    )(qt, qt, kf, vt, u)


def _out_proj_kernel(p_ref, m_ref, a_ref, w_ref, x_ref, o_ref):
    acc = _dot(p_ref[...], w_ref[0:D_POOL, :])
    acc += _dot(m_ref[...], w_ref[D_POOL:D_POOL + D_MLSTM, :])
    acc += _dot(a_ref[...], w_ref[D_POOL + D_MLSTM:, :])
    o_ref[...] = x_ref[...] + acc


def _out_proj(pool_o, mlstm_o, mla_o, w, layer, x, tm, tn):
    S = x.shape[0]
    return pl.pallas_call(
        _out_proj_kernel,
        grid=(S // tm, D_MODEL // tn),
        in_specs=[
            pl.BlockSpec((tm, D_POOL), lambda i, j: (i, 0)),
            pl.BlockSpec((tm, D_MLSTM), lambda i, j: (i, 0)),
            pl.BlockSpec((tm, D_MLA), lambda i, j: (i, 0)),
            pl.BlockSpec((None, D_MODEL, tn), lambda i, j: (layer, 0, j)),
            pl.BlockSpec((tm, tn), lambda i, j: (i, j)),
        ],
        out_specs=pl.BlockSpec((tm, tn), lambda i, j: (i, j)),
        out_shape=jax.ShapeDtypeStruct((S, D_MODEL), F32),
        compiler_params=_cparams(("parallel", "arbitrary")),
        name="out_proj",
    )(pool_o, mlstm_o, mla_o, w, x)


def _pad_lanes(v, offset=0):
    return jnp.zeros((1, LANES), F32).at[0, offset:offset + v.shape[0]].set(v.astype(F32))


_SRC_SIZES = (D_POOL, D_POOL, D_MLSTM, D_MLSTM, D_MLSTM, D_MLSTM, D_MLSTM, MLSTM_GATES,
              Q_LORA, KV_LORA, MLA_ROPE, D_MLA)
(_SRC_PX, _, _, _, _, _, _, _SRC_MG, _SRC_QLAT, _SRC_KVLAT, _SRC_KROPE, _SRC_AZ) = (
    int(o) for o in np.concatenate([[0], np.cumsum(_SRC_SIZES)])[:-1])
W_IN_MOVES = (
    (OFF_QLAT, _SRC_QLAT, Q_LORA), (OFF_KVLAT, _SRC_KVLAT, KV_LORA), (OFF_AZ, _SRC_AZ, D_MLA),
    (OFF_PX, _SRC_PX, 2 * D_POOL + 5 * D_MLSTM),
)
W_IN_TILE = 256
W_IN_MISC_TILE = OFF_MISC // W_IN_TILE


def _w_in_src_row(j):
    unit = 16
    row = jnp.int32(_SRC_KROPE // unit)
    for dst, src, width in reversed(W_IN_MOVES):
        row = jnp.where(j < (dst + width) // W_IN_TILE, src // unit + (j - dst // W_IN_TILE) * (W_IN_TILE // unit), row)
    return row * unit


def _w_in_prep_kernel(rows_ref, gate_rows_ref, o_ref):
    j = pl.program_id(1)

    @pl.when(j != W_IN_MISC_TILE)
    def _():
        o_ref[...] = rows_ref[0].T.astype(o_ref.dtype)

    @pl.when(j == W_IN_MISC_TILE)
    def _():
        pad = jnp.zeros((W_IN_TILE - MLA_ROPE - MLSTM_GATES, rows_ref.shape[2]), rows_ref.dtype)
        tile = jnp.concatenate([rows_ref[0, 0:MLA_ROPE, :], gate_rows_ref[0, 0:MLSTM_GATES, :], pad], axis=0)
        o_ref[...] = tile.T.astype(o_ref.dtype)


def _prep_w_in(w_in):
    wt = jnp.swapaxes(w_in, 1, 2)
    n_layers, _, K = wt.shape
    elem = pl.Element
    return pl.pallas_call(
        _w_in_prep_kernel,
        grid=(n_layers, N_IN_PAD // W_IN_TILE),
        in_specs=[pl.BlockSpec((elem(1), elem(W_IN_TILE), elem(K)), lambda l, j: (l, _w_in_src_row(j), 0)),
                  pl.BlockSpec((elem(1), elem(W_IN_TILE), elem(K)), lambda l, j: (l, _SRC_MG, 0))],
        out_specs=pl.BlockSpec((None, K, W_IN_TILE), lambda l, j: (l, 0, j)),
        out_shape=jax.ShapeDtypeStruct((n_layers, K, N_IN_PAD), BF16),
        compiler_params=_cparams(("parallel", "parallel")),
        name="w_in_prep",
    )(wt, wt)


def _w_uq_prep_kernel(w_ref, o_ref):
    for hd in range(MLA_HEADS):
        o_ref[:, hd * QK_PAD: hd * QK_PAD + MLA_QK] = w_ref[:, hd * MLA_QK: (hd + 1) * MLA_QK].astype(o_ref.dtype)
        o_ref[:, hd * QK_PAD + MLA_QK: (hd + 1) * QK_PAD] = jnp.zeros((o_ref.shape[0], QK_PAD - MLA_QK), o_ref.dtype)


def _prep_weight(body, w, n_out, tr, name):
    n_layers, K, N = w.shape
    return pl.pallas_call(
        body,
        grid=(n_layers, K // tr),
        in_specs=[pl.BlockSpec((None, tr, N), lambda l, i: (l, i, 0))],
        out_specs=pl.BlockSpec((None, tr, n_out), lambda l, i: (l, i, 0)),
        out_shape=jax.ShapeDtypeStruct((n_layers, K, n_out), BF16),
        compiler_params=_cparams(("parallel", "parallel")),
        name=name,
    )(w)


def _prep_layer(norm_g, gate_bias, pool_w, pool_scale, mlstm_norm_g, qlat_g, kvlat_g, w_ukv, qn_g, qr_g, kn_g, kr_g):
    wkv = w_ukv.reshape(KV_LORA, MLA_HEADS, MLA_NOPE + MLA_V)
    w_ukv_p = jnp.concatenate([wkv[:, :, :MLA_NOPE].reshape(KV_LORA, -1),
                               wkv[:, :, MLA_NOPE:].reshape(KV_LORA, -1)], axis=1).astype(BF16)
    return dict(
        norm_g=norm_g.reshape(1, D_MODEL),
        bias_misc=_pad_lanes(gate_bias, MISC_GATE0),
        pool_w=pool_w.astype(BF16), pool_scale=pool_scale.reshape(1, D_POOL),
        mlstm_norm_g=mlstm_norm_g.reshape(1, D_MLSTM),
        qlat_g=qlat_g.reshape(1, Q_LORA),
        kvlat_g=kvlat_g.reshape(1, KV_LORA), w_ukv=w_ukv_p,
        qn_g=qn_g.reshape(1, MLA_NOPE), qr_g=_pad_lanes(qr_g),
        kn_g=kn_g.reshape(1, MLA_NOPE), kr_g=_pad_lanes(kr_g),
    )


def _rope_tables(S):
    pos = jnp.arange(S, dtype=F32)
    inv_freq = ROPE_THETA ** (-(jnp.arange(0, MLA_ROPE, 2, dtype=F32) / MLA_ROPE))
    ang = pos[:, None] * inv_freq[None, :]
    cos, sin = jnp.cos(ang), jnp.sin(ang)
    pad = jnp.zeros((S, LANES - MLA_ROPE), F32)
    return jnp.concatenate([cos, cos, pad], axis=1), jnp.concatenate([-sin, sin, pad], axis=1)


ATTN_SLOTS = 2


def _tiles(S):
    t = lambda pref: min(pref, S)
    n_chunks = S // t(512)
    slots = min(ATTN_SLOTS, n_chunks)
    return dict(tm_in=t(512), tn_in=768, tm_pool=t(512), tm_mo=t(512), tm_q=t(512), q_heads=4,
                tk=t(512), tq=t(512), attn_group=min(16, n_chunks), attn_slots=slots, tm_out=t(512), tn_out=1024)


def _layer(x, layer, p, big, cos_t, sin_t):
    S = x.shape[0]
    t = _tiles(S)
    u = _in_proj(x, p["norm_g"], big["w_in"], layer, t["tm_in"], t["tn_in"])
    pool_o = _pool(u, p["pool_w"], p["pool_scale"], t["tm_pool"])
    hf, hb = _mlstm(u, p["bias_misc"])
    mlstm_o = _mlstm_out(hf, hb, u, p["mlstm_norm_g"], t["tm_mo"])
    qt = _q_up(u, p["qlat_g"], big["w_uq"], layer, p["qn_g"], p["qr_g"], cos_t, sin_t, t["tm_q"], t["q_heads"])
    kf, vt = _kv_up(u, p["kvlat_g"], p["w_ukv"], p["kn_g"], p["kr_g"], cos_t, sin_t, t["tk"])
    mla_o = _attn(qt, kf, vt, u, t["tq"], t["attn_group"], t["attn_slots"])
    return _out_proj(pool_o, mlstm_o, mla_o, big["w_out"], layer, x, t["tm_out"], t["tn_out"])


def _trunk(x, layers, big, cos_t, sin_t):
    x = x[0]
    for layer, p in enumerate(layers):
        x = _layer(x, layer, p, big, cos_t, sin_t)
    return x[None]


def kernel(x_prompt, x_sample, norm_g, w_in, gate_bias, pool_w, pool_scale, mlstm_norm_g,
           qlat_g, w_uq, kvlat_g, w_ukv, qn_g, qr_g, kn_g, kr_g, w_out):
    small = (norm_g, gate_bias, pool_w, pool_scale, mlstm_norm_g, qlat_g, kvlat_g, w_ukv, qn_g, qr_g, kn_g, kr_g)
    layers = [_prep_layer(*[w[l] for w in small]) for l in range(norm_g.shape[0])]
    big = dict(w_in=_prep_w_in(w_in),
               w_uq=_prep_weight(_w_uq_prep_kernel, w_uq, MLA_HEADS * QK_PAD, 512, "w_uq_prep"),
               w_out=w_out.astype(BF16))
    cos_t, sin_t = _rope_tables(max(x_prompt.shape[1], x_sample.shape[1]))
    return (_trunk(x_prompt, layers, big, cos_t, sin_t), _trunk(x_sample, layers, big, cos_t, sin_t))
```

```python
import functools

import numpy as np
import jax
import jax.numpy as jnp
from jax import lax
from jax.experimental import pallas as pl
from jax.experimental.pallas import tpu as pltpu

D_MODEL = 4096
D_POOL = 1024
D_MLSTM = 1024
D_MLA = 2048
POOL_WINDOWS = (2, 4, 8, 16)
POOL_GW = 256
MLSTM_HEADS = 4
MLSTM_HD = 256
MLSTM_CHUNK = 128
MLSTM_GATES = 16
MLA_HEADS = 16
MLA_V = 128
MLA_NOPE = 128
MLA_ROPE = 64
MLA_QK = 192
Q_LORA = 1536
KV_LORA = 512
ROPE_THETA = 10000.0
NORM_EPS = 1e-6

LANES = 128
QK_PAD = 256
V_ONES = 16
V_AUG = MLA_V + V_ONES
LOG2_E = 1.4426950408889634
VMEM_LIMIT = 56 * 1024 * 1024

OFF_QLAT = 0
OFF_KVLAT = 1536
OFF_AZ = 2048
OFF_PX = 4096
OFF_PZ = 5120
OFF_MQ = 6144
OFF_MK = 7168
OFF_MV = 8192
OFF_MO = 9216
OFF_MZ = 10240
OFF_MISC = 11264
N_IN_PAD = 11520
MISC_GATE0 = MLA_ROPE

NEG_BIG = -1e30

F32 = jnp.float32
BF16 = jnp.bfloat16


def _cparams(sem):
    return pltpu.CompilerParams(dimension_semantics=sem, vmem_limit_bytes=VMEM_LIMIT)


def _sigmoid(x):
    return 1.0 / (1.0 + jnp.exp(-x))


def _silu(x):
    return x * _sigmoid(x)


def _dot(a, b):
    return jnp.dot(a, b, preferred_element_type=F32)


def _dot_nt(a, b):
    return lax.dot_general(a, b, (((1,), (1,)), ((), ())), preferred_element_type=F32)


def _dot_tn(a, b):
    return lax.dot_general(a, b, (((0,), (0,)), ((), ())), preferred_element_type=F32)


def _norm_matmul_kernel(x_ref, g_ref, w_ref, o_ref, h_ref):
    @pl.when(pl.program_id(1) == 0)
    def _():
        x = x_ref[...]
        ms = jnp.mean(x * x, axis=-1, keepdims=True)
        h_ref[...] = (x * lax.rsqrt(ms + NORM_EPS) * g_ref[...]).astype(BF16)

    o_ref[...] = _dot(h_ref[...], w_ref[...]).astype(o_ref.dtype)


def _in_proj(x, g, w, layer, tm, tn):
    S, K = x.shape
    N = w.shape[2]
    return pl.pallas_call(
        _norm_matmul_kernel,
        grid=(S // tm, N // tn),
        in_specs=[
            pl.BlockSpec((tm, K), lambda i, j: (i, 0)),
            pl.BlockSpec((1, K), lambda i, j: (0, 0)),
            pl.BlockSpec((None, K, tn), lambda i, j: (layer, 0, j)),
        ],
        out_specs=pl.BlockSpec((tm, tn), lambda i, j: (i, j)),
        out_shape=jax.ShapeDtypeStruct((S, N), F32),
        scratch_shapes=[pltpu.VMEM((tm, K), BF16)],
        compiler_params=_cparams(("parallel", "arbitrary")),
        name="in_proj",
    )(x, g, w)


POOL_HALO = 8


def _pool_kernel(prev_ref, cur_ref, next_ref, z_ref, w_ref, scale_ref, o_ref, *, seq_len):
    i = pl.program_id(0)
    tm = cur_ref.shape[0]
    rows = tm + 2 * POOL_HALO
    prev = jnp.where(i > 0, prev_ref[...], 0.0)
    nxt = jnp.where(i < pl.num_programs(0) - 1, next_ref[...], 0.0)
    cur = cur_ref[...]
    xall = jnp.concatenate([prev, cur, nxt], axis=0)
    pos = i * tm + lax.broadcasted_iota(jnp.int32, (tm, 1), 0)
    for g, win in enumerate(POOL_WINDOWS):
        half = win // 2
        sl = slice(g * POOL_GW, (g + 1) * POOL_GW)
        acc = xall[:, sl]
        span = 1
        while span < win:
            acc = acc + pltpu.roll(acc, span, axis=0)
            span *= 2
        shift = half - 1
        if shift:
            acc = pltpu.roll(acc, rows - shift, axis=0)
        wsum = acc[POOL_HALO:POOL_HALO + tm]
        cnt = jnp.minimum(pos + half, seq_len) - jnp.maximum(pos - half, 0)
        diff = wsum / cnt.astype(F32) - cur[:, sl]
        y = _dot(diff.astype(BF16), w_ref[g])
        o_ref[:, sl] = (y * scale_ref[:, sl] * _silu(z_ref[:, sl])).astype(o_ref.dtype)


def _pool(u, pool_w, pool_scale, tm):
    S = u.shape[0]
    hb = tm // POOL_HALO
    n_halo = S // POOL_HALO
    cpx = OFF_PX // D_POOL
    return pl.pallas_call(
        functools.partial(_pool_kernel, seq_len=S),
        grid=(S // tm,),
        in_specs=[
            pl.BlockSpec((POOL_HALO, D_POOL), lambda i: (jnp.maximum(i * hb - 1, 0), cpx)),
            pl.BlockSpec((tm, D_POOL), lambda i: (i, cpx)),
            pl.BlockSpec((POOL_HALO, D_POOL), lambda i: (jnp.minimum((i + 1) * hb, n_halo - 1), cpx)),
            pl.BlockSpec((tm, D_POOL), lambda i: (i, OFF_PZ // D_POOL)),
            pl.BlockSpec((len(POOL_WINDOWS), POOL_GW, POOL_GW), lambda i: (0, 0, 0)),
            pl.BlockSpec((1, D_POOL), lambda i: (0, 0)),
        ],
        out_specs=pl.BlockSpec((tm, D_POOL), lambda i: (i, 0)),
        out_shape=jax.ShapeDtypeStruct((S, D_POOL), BF16),
        compiler_params=_cparams(("parallel",)),
        name="pool",
    )(u, u, u, u, pool_w, pool_scale)


def _split3(x):
    hi = x.astype(BF16)
    r1 = x - hi.astype(F32)
    mid = r1.astype(BF16)
    lo = (r1 - mid.astype(F32)).astype(BF16)
    return hi, mid, lo


def _log_sigmoid(x):
    return jnp.minimum(x, 0.0) - jnp.log(1.0 + jnp.exp(-jnp.abs(x)))


def _lane_tile(x, reps):
    return jnp.concatenate([x] * reps, axis=1)


def _mlstm_chain(q, k, v, i_rep, b_rep, pm_rep, i_row, b_row, c_old, m, reverse):
    L = q.shape[0]
    row = lax.broadcasted_iota(jnp.int32, (L, L), 0)
    col = lax.broadcasted_iota(jnp.int32, (L, L), 1)
    keep = (col >= row) if reverse else (col <= row)
    b_last = b_rep[0:1] if reverse else b_rep[L - 1:L]

    m_t = b_rep + jnp.maximum(m, pm_rep)
    qb = q.astype(BF16)
    ks = k * (MLSTM_HD ** -0.5)
    v_aug = jnp.concatenate([v.astype(BF16), jnp.ones((L, LANES), BF16)], axis=1)
    dexp = jnp.exp(jnp.where(keep, (b_rep - m_t) - (b_row - i_row), NEG_BIG))
    w = dexp * _dot_nt(qb, ks.astype(BF16))
    ei = jnp.exp(b_rep + m - m_t)
    numden = (_lane_tile(ei, c_old.shape[1] // LANES) * _dot(qb, c_old.astype(BF16))
              + _dot(w.astype(BF16), v_aug))
    den = jnp.maximum(jnp.abs(numden[:, MLSTM_HD:]), jnp.exp(-m_t))
    h = numden[:, :MLSTM_HD] / _lane_tile(den, MLSTM_HD // LANES)

    a_rep = b_last - b_rep + i_rep
    m_new = jnp.maximum(b_last + m, jnp.max(a_rep, axis=0, keepdims=True))
    decay = jnp.exp(b_last + m - m_new)
    ek = _lane_tile(jnp.exp(a_rep - m_new), MLSTM_HD // LANES) * ks
    c_new = _lane_tile(decay, c_old.shape[1] // LANES) * c_old + _dot_tn(ek.astype(BF16), v_aug)
    return h, c_new, m_new


def _mlstm_kernel(qf_ref, kf_ref, vf_ref, gf_ref, qb_ref, kb_ref, vb_ref, gb_ref, bias_ref,
                  hf_ref, hb_ref, c_ref, m_ref):
    @pl.when(pl.program_id(0) == 0)
    def _():
        c_ref[...] = jnp.zeros_like(c_ref)
        m_ref[...] = jnp.zeros_like(m_ref)

    L = gf_ref.shape[0]
    row = lax.broadcasted_iota(jnp.int32, (L, L), 0)
    col = lax.broadcasted_iota(jnp.int32, (L, L), 1)
    dirs = ((qf_ref, kf_ref, vf_ref, gf_ref, hf_ref, False), (qb_ref, kb_ref, vb_ref, gb_ref, hb_ref, True))
    gates = []
    for d, (q_ref, k_ref, v_ref, g_ref, h_ref, reverse) in enumerate(dirs):
        g = g_ref[...] + bias_ref[...]
        lane_i0 = MISC_GATE0 + (2 * d) * MLSTM_HEADS
        tri = jnp.where((col >= row) if reverse else (col <= row), 1.0, 0.0).astype(BF16)
        b_all = sum(_dot(tri, p) for p in _split3(_log_sigmoid(g)))
        pm_all = g - pltpu.roll(b_all, LANES - MLSTM_HEADS, axis=1)
        shift = 1
        while shift < L:
            if reverse:
                moved = jnp.where(row < L - shift, pltpu.roll(pm_all, L - shift, axis=0), NEG_BIG)
            else:
                moved = jnp.where(row >= shift, pltpu.roll(pm_all, shift, axis=0), NEG_BIG)
            pm_all = jnp.maximum(pm_all, moved)
            shift *= 2
        gt = g.T
        bt = b_all.T
        for hd in range(MLSTM_HEADS):
            lane_i = lane_i0 + hd
            lane_f = lane_i0 + MLSTM_HEADS + hd
            lane_rep = lambda x, lane: jnp.broadcast_to(x[:, lane:lane + 1], (L, LANES))
            gates.append((lane_rep(g, lane_i), lane_rep(b_all, lane_f), lane_rep(pm_all, lane_i),
                          gt[lane_i:lane_i + 1, :], bt[lane_f:lane_f + 1, :]))
    results = []
    for d, (q_ref, k_ref, v_ref, g_ref, h_ref, reverse) in enumerate(dirs):
        for hd in range(MLSTM_HEADS):
            sl = slice(hd * MLSTM_HD, (hd + 1) * MLSTM_HD)
            idx = d * MLSTM_HEADS + hd
            h, c_new, m_new = _mlstm_chain(q_ref[:, sl], k_ref[:, sl], v_ref[:, sl], *gates[idx],
                                           c_ref[idx], m_ref[idx], reverse)
            results.append((h_ref, sl, idx, h, c_new, m_new))
    for h_ref, sl, idx, h, c_new, m_new in results:
        h_ref[:, sl] = h
        c_ref[idx] = c_new
        m_ref[idx] = m_new


def _mlstm(u, bias_misc):
    S = u.shape[0]
    L = MLSTM_CHUNK
    nc = S // L
    fwd = lambda c: c
    bwd = lambda c: nc - 1 - c

    def spec(off, width, cmap):
        return pl.BlockSpec((L, width), lambda c: (cmap(c), off // width))

    in_specs = []
    for cmap in (fwd, bwd):
        in_specs += [spec(OFF_MQ, D_MLSTM, cmap), spec(OFF_MK, D_MLSTM, cmap), spec(OFF_MV, D_MLSTM, cmap),
                     spec(OFF_MISC, LANES, cmap)]
    in_specs.append(pl.BlockSpec((1, LANES), lambda c: (0, 0)))
    nstate = 2 * MLSTM_HEADS
    return pl.pallas_call(
        _mlstm_kernel,
        grid=(nc,),
        in_specs=in_specs,
        out_specs=[pl.BlockSpec((L, D_MLSTM), lambda c: (c, 0)),
                   pl.BlockSpec((L, D_MLSTM), lambda c: (nc - 1 - c, 0))],
        out_shape=[jax.ShapeDtypeStruct((S, D_MLSTM), F32)] * 2,
        scratch_shapes=[pltpu.VMEM((nstate, MLSTM_HD, MLSTM_HD + LANES), F32),
                        pltpu.VMEM((nstate, 1, LANES), F32)],
        compiler_params=_cparams(("arbitrary",)),
        name="mlstm",
    )(u, u, u, u, u, u, u, u, bias_misc)


def _mlstm_out_kernel(hf_ref, hb_ref, o_ref, z_ref, g_ref, out_ref):
    for hd in range(MLSTM_HEADS):
        sl = slice(hd * MLSTM_HD, (hd + 1) * MLSTM_HD)
        h = hf_ref[:, sl] + hb_ref[:, sl]
        ms = jnp.mean(h * h, axis=-1, keepdims=True)
        hn = h * lax.rsqrt(ms + NORM_EPS) * g_ref[:, sl]
        out_ref[:, sl] = (_sigmoid(o_ref[:, sl]) * hn * _silu(z_ref[:, sl])).astype(out_ref.dtype)


def _mlstm_out(hf, hb, u, norm_g, tm):
    S = hf.shape[0]
    return pl.pallas_call(
        _mlstm_out_kernel,
        grid=(S // tm,),
        in_specs=[
            pl.BlockSpec((tm, D_MLSTM), lambda i: (i, 0)),
            pl.BlockSpec((tm, D_MLSTM), lambda i: (i, 0)),
            pl.BlockSpec((tm, D_MLSTM), lambda i: (i, OFF_MO // D_MLSTM)),
            pl.BlockSpec((tm, D_MLSTM), lambda i: (i, OFF_MZ // D_MLSTM)),
            pl.BlockSpec((1, D_MLSTM), lambda i: (0, 0)),
        ],
        out_specs=pl.BlockSpec((tm, D_MLSTM), lambda i: (i, 0)),
        out_shape=jax.ShapeDtypeStruct((S, D_MLSTM), BF16),
        compiler_params=_cparams(("parallel",)),
        name="mlstm_out",
    )(hf, hb, u, u, norm_g)


def _rope128(x, g, cos_ref, sin_ref):
    lane = lax.broadcasted_iota(jnp.int32, x.shape, 1)
    x = jnp.where(lane < MLA_ROPE, x, 0.0)
    ms = jnp.sum(x * x, axis=-1, keepdims=True) * (1.0 / MLA_ROPE)
    xn = x * lax.rsqrt(ms + NORM_EPS) * g
    half = MLA_ROPE // 2
    partner = jnp.where(lane < half, pltpu.roll(xn, LANES - half, axis=1), pltpu.roll(xn, half, axis=1))
    return xn * cos_ref[...] + partner * sin_ref[...]


def _q_up_kernel(x_ref, g_ref, w_ref, qn_ref, qr_ref, cos_ref, sin_ref, o_ref, h_ref, acc_ref,
                 *, heads_per_tile, n_col, n_tiles):
    s = pl.program_id(0)

    @pl.when(jnp.logical_and(s % n_col == 0, s < n_tiles))
    def _():
        x = x_ref[...]
        ms = jnp.mean(x * x, axis=-1, keepdims=True)
        h_ref[...] = (x * lax.rsqrt(ms + NORM_EPS) * g_ref[...]).astype(BF16)

    @pl.when(s == 0)
    def _():
        acc_ref[1] = jnp.zeros(acc_ref.shape[1:], acc_ref.dtype)

    scale = MLA_QK ** -0.5 * LOG2_E

    def step(fill, drain):
        for hh in range(heads_per_tile):
            cols = slice(hh * QK_PAD, (hh + 1) * QK_PAD)
            acc_ref[fill, :, cols] = _dot(h_ref[...], w_ref[:, cols])
        for hh in range(heads_per_tile):
            nope = acc_ref[drain, :, hh * QK_PAD: hh * QK_PAD + MLA_NOPE]
            ms = jnp.mean(nope * nope, axis=-1, keepdims=True)
            nope = nope * lax.rsqrt(ms + NORM_EPS) * qn_ref[...]
            pe = _rope128(acc_ref[drain, :, hh * QK_PAD + MLA_NOPE: (hh + 1) * QK_PAD], qr_ref[...], cos_ref, sin_ref)
            o_ref[hh * QK_PAD: hh * QK_PAD + MLA_NOPE, :] = (nope * scale).astype(o_ref.dtype).T
            o_ref[hh * QK_PAD + MLA_NOPE: (hh + 1) * QK_PAD, :] = (pe * scale).astype(o_ref.dtype).T

    @pl.when(s % 2 == 0)
    def _():
        step(0, 1)

    @pl.when(s % 2 == 1)
    def _():
        step(1, 0)


def _q_up(u, g, w, layer, qn_g, qr_g, cos_t, sin_t, tm, heads_per_tile):
    S = u.shape[0]
    tn = heads_per_tile * QK_PAD
    N = MLA_HEADS * QK_PAD
    n_col = N // tn
    n_tiles = (S // tm) * n_col
    fill_row = lambda s: jnp.minimum(s, n_tiles - 1) // n_col
    fill_col = lambda s: jnp.minimum(s, n_tiles - 1) % n_col
    drain_row = lambda s: jnp.maximum(s - 1, 0) // n_col
    drain_col = lambda s: jnp.maximum(s - 1, 0) % n_col
    return pl.pallas_call(
        functools.partial(_q_up_kernel, heads_per_tile=heads_per_tile, n_col=n_col, n_tiles=n_tiles),
        grid=(n_tiles + 1,),
        in_specs=[
            pl.BlockSpec((tm, Q_LORA), lambda s: (fill_row(s), OFF_QLAT // Q_LORA)),
            pl.BlockSpec((1, Q_LORA), lambda s: (0, 0)),
            pl.BlockSpec((None, Q_LORA, tn), lambda s: (layer, 0, fill_col(s))),
            pl.BlockSpec((1, LANES), lambda s: (0, 0)),
            pl.BlockSpec((1, LANES), lambda s: (0, 0)),
            pl.BlockSpec((tm, LANES), lambda s: (drain_row(s), 0)),
            pl.BlockSpec((tm, LANES), lambda s: (drain_row(s), 0)),
        ],
        out_specs=pl.BlockSpec((tn, tm), lambda s: (drain_col(s), drain_row(s))),
        out_shape=jax.ShapeDtypeStruct((N, S), BF16),
        scratch_shapes=[pltpu.VMEM((tm, Q_LORA), BF16), pltpu.VMEM((2, tm, tn), F32)],
        compiler_params=_cparams(("arbitrary",)),
        name="q_up",
    )(u, g, w, qn_g, qr_g, cos_t, sin_t)


def _kv_up_kernel(x_ref, misc_ref, g_ref, w_ref, kn_ref, kr_ref, cos_ref, sin_ref, k_ref, vt_ref):
    x = x_ref[...]
    ms = jnp.mean(x * x, axis=-1, keepdims=True)
    h = (x * lax.rsqrt(ms + NORM_EPS) * g_ref[...]).astype(BF16)
    acc = _dot(h, w_ref[...])
    pe = _rope128(misc_ref[...], kr_ref[...], cos_ref, sin_ref).astype(k_ref.dtype)
    for hd in range(MLA_HEADS):
        nope = acc[:, hd * MLA_NOPE: (hd + 1) * MLA_NOPE]
        ms = jnp.mean(nope * nope, axis=-1, keepdims=True)
        k_ref[:, hd * QK_PAD: hd * QK_PAD + MLA_NOPE] = (nope * lax.rsqrt(ms + NORM_EPS) * kn_ref[...]).astype(k_ref.dtype)
        k_ref[:, hd * QK_PAD + MLA_NOPE: (hd + 1) * QK_PAD] = pe
        v = acc[:, MLA_HEADS * MLA_NOPE + hd * MLA_V: MLA_HEADS * MLA_NOPE + (hd + 1) * MLA_V]
        vt_ref[hd * V_AUG: hd * V_AUG + MLA_V, :] = v.astype(vt_ref.dtype).T
        vt_ref[hd * V_AUG + MLA_V: (hd + 1) * V_AUG, :] = jnp.ones((V_ONES, vt_ref.shape[1]), vt_ref.dtype)


def _kv_up(u, g, w, kn_g, kr_g, cos_t, sin_t, tk):
    S = u.shape[0]
    return pl.pallas_call(
        _kv_up_kernel,
        grid=(S // tk,),
        in_specs=[
            pl.BlockSpec((tk, KV_LORA), lambda i: (i, OFF_KVLAT // KV_LORA)),
            pl.BlockSpec((tk, LANES), lambda i: (i, OFF_MISC // LANES)),
            pl.BlockSpec((1, KV_LORA), lambda i: (0, 0)),
            pl.BlockSpec((KV_LORA, 2 * D_MLA), lambda i: (0, 0)),
            pl.BlockSpec((1, LANES), lambda i: (0, 0)),
            pl.BlockSpec((1, LANES), lambda i: (0, 0)),
            pl.BlockSpec((tk, LANES), lambda i: (i, 0)),
            pl.BlockSpec((tk, LANES), lambda i: (i, 0)),
        ],
        out_specs=[pl.BlockSpec((tk, MLA_HEADS * QK_PAD), lambda i: (i, 0)),
                   pl.BlockSpec((None, MLA_HEADS * V_AUG, tk), lambda i: (i, 0, 0))],
        out_shape=[jax.ShapeDtypeStruct((S, MLA_HEADS * QK_PAD), BF16),
                   jax.ShapeDtypeStruct((S // tk, MLA_HEADS * V_AUG, tk), BF16)],
        compiler_params=_cparams(("parallel",)),
        name="kv_up",
    )(u, u, g, w, kn_g, kr_g, cos_t, sin_t)


def _attn_kernel(qt_ref, qnext_ref, k_ref, vt_ref, z_ref, o_ref, s_ref, acc_ref, cmax_ref, fin_ref,
                 *, group_size, n_q, n_tiles):
    n_chunks, _, tk = vt_ref.shape
    tq = qt_ref.shape[1]
    n_slots = s_ref.shape[0]
    step = pl.program_id(0)
    q_tile = jnp.minimum(step, n_tiles - 1) % n_q

    def scores(q_ref, j, slot):
        k = k_ref[pl.ds(pl.multiple_of(j * tk, tk), tk), :]
        st = _dot(k, q_ref[...])
        s_ref[slot] = st
        return jnp.max(st, axis=0, keepdims=True)

    def accumulate(j, slot, m, cmax):
        m_new = jnp.maximum(m, cmax)
        alpha = jnp.exp2(m - m_new)
        p = jnp.exp2(s_ref[slot] - m_new).astype(BF16)
        acc_ref[...] = alpha * acc_ref[...] + _dot(vt_ref[j], p)
        return m_new

    def group(j0, m, cmax, last):
        for u in range(group_size):
            if last and u == group_size - 1:
                cnext = scores(qnext_ref, 0, 0)
            else:
                cnext = scores(qt_ref, j0 + u + 1, (u + 1) % n_slots)
            m = accumulate(j0 + u, u % n_slots, m, cmax)
            cmax = cnext
        return m, cmax

    @pl.when(step == 0)
    def _():
        fin_ref[...] = jnp.ones_like(fin_ref)

    @pl.when(q_tile == 0)
    def _():
        cmax_ref[...] = scores(qt_ref, 0, 0)

    o = (fin_ref[0:MLA_V, :] / fin_ref[MLA_V:MLA_V + 1, :]).T
    o_ref[...] = (o * _silu(z_ref[...])).astype(o_ref.dtype)

    acc_ref[...] = jnp.zeros_like(acc_ref)
    m = jnp.full((1, tq), NEG_BIG, F32)
    n_groups = n_chunks // group_size
    m, cmax = lax.fori_loop(0, n_groups - 1, lambda g, c: group(g * group_size, c[0], c[1], False),
                            (m, cmax_ref[...]))
    _, cmax_ref[...] = group((n_groups - 1) * group_size, m, cmax, True)
    fin_ref[...] = acc_ref[...]


def _attn(qt, kf, vt, u, tq, group_size, n_slots):
    S = kf.shape[0]
    n_chunks, _, tk = vt.shape
    n_q = S // tq
    n_tiles = MLA_HEADS * n_q
    assert n_chunks % group_size == 0 and group_size % n_slots == 0 and n_slots >= 2
    tile = lambda s: jnp.minimum(s, n_tiles - 1)
    done = lambda s: jnp.maximum(s - 1, 0)
    return pl.pallas_call(
        functools.partial(_attn_kernel, group_size=group_size, n_q=n_q, n_tiles=n_tiles),
        grid=(n_tiles + 1,),
        in_specs=[
            pl.BlockSpec((QK_PAD, tq), lambda s: (tile(s) // n_q, tile(s) % n_q)),
            pl.BlockSpec((QK_PAD, tq), lambda s: (tile(s) // n_q, jnp.minimum(tile(s) % n_q + 1, n_q - 1))),
            pl.BlockSpec((S, QK_PAD), lambda s: (0, tile(s) // n_q)),
            pl.BlockSpec((n_chunks, V_AUG, tk), lambda s: (0, tile(s) // n_q, 0)),
            pl.BlockSpec((tq, MLA_V), lambda s: (done(s) % n_q, OFF_AZ // MLA_V + done(s) // n_q)),
        ],
        out_specs=pl.BlockSpec((tq, MLA_V), lambda s: (done(s) % n_q, done(s) // n_q)),
        out_shape=jax.ShapeDtypeStruct((S, D_MLA), BF16),
        scratch_shapes=[pltpu.VMEM((n_slots, tk, tq), F32), pltpu.VMEM((V_AUG, tq), F32),
                        pltpu.VMEM((1, tq), F32), pltpu.VMEM((V_AUG, tq), F32)],
        compiler_params=_cparams(("arbitrary",)),
        name="attn",
    )(qt, qt, kf, vt, u)


def _out_proj_kernel(p_ref, m_ref, a_ref, w_ref, x_ref, o_ref):
    acc = _dot(p_ref[...], w_ref[0:D_POOL, :])
    acc += _dot(m_ref[...], w_ref[D_POOL:D_POOL + D_MLSTM, :])
    acc += _dot(a_ref[...], w_ref[D_POOL + D_MLSTM:, :])
    o_ref[...] = x_ref[...] + acc


def _out_proj(pool_o, mlstm_o, mla_o, w, layer, x, tm, tn):
    S = x.shape[0]
    return pl.pallas_call(
        _out_proj_kernel,
        grid=(S // tm, D_MODEL // tn),
        in_specs=[
            pl.BlockSpec((tm, D_POOL), lambda i, j: (i, 0)),
            pl.BlockSpec((tm, D_MLSTM), lambda i, j: (i, 0)),
            pl.BlockSpec((tm, D_MLA), lambda i, j: (i, 0)),
            pl.BlockSpec((None, D_MODEL, tn), lambda i, j: (layer, 0, j)),
            pl.BlockSpec((tm, tn), lambda i, j: (i, j)),
        ],
        out_specs=pl.BlockSpec((tm, tn), lambda i, j: (i, j)),
        out_shape=jax.ShapeDtypeStruct((S, D_MODEL), F32),
        compiler_params=_cparams(("parallel", "arbitrary")),
        name="out_proj",
    )(pool_o, mlstm_o, mla_o, w, x)


def _pad_lanes(v, offset=0):
    return jnp.zeros((1, LANES), F32).at[0, offset:offset + v.shape[0]].set(v.astype(F32))


_SRC_SIZES = (D_POOL, D_POOL, D_MLSTM, D_MLSTM, D_MLSTM, D_MLSTM, D_MLSTM, MLSTM_GATES,
              Q_LORA, KV_LORA, MLA_ROPE, D_MLA)
(_SRC_PX, _, _, _, _, _, _, _SRC_MG, _SRC_QLAT, _SRC_KVLAT, _SRC_KROPE, _SRC_AZ) = (
    int(o) for o in np.concatenate([[0], np.cumsum(_SRC_SIZES)])[:-1])
W_IN_MOVES = (
    (OFF_QLAT, _SRC_QLAT, Q_LORA), (OFF_KVLAT, _SRC_KVLAT, KV_LORA), (OFF_AZ, _SRC_AZ, D_MLA),
    (OFF_PX, _SRC_PX, 2 * D_POOL + 5 * D_MLSTM),
)
W_IN_TILE = 256
W_IN_MISC_TILE = OFF_MISC // W_IN_TILE


def _w_in_src_row(j):
    unit = 16
    row = jnp.int32(_SRC_KROPE // unit)
    for dst, src, width in reversed(W_IN_MOVES):
        row = jnp.where(j < (dst + width) // W_IN_TILE, src // unit + (j - dst // W_IN_TILE) * (W_IN_TILE // unit), row)
    return row * unit


def _w_in_prep_kernel(rows_ref, gate_rows_ref, o_ref):
    j = pl.program_id(1)

    @pl.when(j != W_IN_MISC_TILE)
    def _():
        o_ref[...] = rows_ref[0].T.astype(o_ref.dtype)

    @pl.when(j == W_IN_MISC_TILE)
    def _():
        pad = jnp.zeros((W_IN_TILE - MLA_ROPE - MLSTM_GATES, rows_ref.shape[2]), rows_ref.dtype)
        tile = jnp.concatenate([rows_ref[0, 0:MLA_ROPE, :], gate_rows_ref[0, 0:MLSTM_GATES, :], pad], axis=0)
        o_ref[...] = tile.T.astype(o_ref.dtype)


def _prep_w_in(w_in):
    wt = jnp.swapaxes(w_in, 1, 2)
    n_layers, _, K = wt.shape
    elem = pl.Element
    return pl.pallas_call(
        _w_in_prep_kernel,
        grid=(n_layers, N_IN_PAD // W_IN_TILE),
        in_specs=[pl.BlockSpec((elem(1), elem(W_IN_TILE), elem(K)), lambda l, j: (l, _w_in_src_row(j), 0)),
                  pl.BlockSpec((elem(1), elem(W_IN_TILE), elem(K)), lambda l, j: (l, _SRC_MG, 0))],
        out_specs=pl.BlockSpec((None, K, W_IN_TILE), lambda l, j: (l, 0, j)),
        out_shape=jax.ShapeDtypeStruct((n_layers, K, N_IN_PAD), BF16),
        compiler_params=_cparams(("parallel", "parallel")),
        name="w_in_prep",
    )(wt, wt)


def _w_uq_prep_kernel(w_ref, o_ref):
    for hd in range(MLA_HEADS):
        o_ref[:, hd * QK_PAD: hd * QK_PAD + MLA_QK] = w_ref[:, hd * MLA_QK: (hd + 1) * MLA_QK].astype(o_ref.dtype)
        o_ref[:, hd * QK_PAD + MLA_QK: (hd + 1) * QK_PAD] = jnp.zeros((o_ref.shape[0], QK_PAD - MLA_QK), o_ref.dtype)


def _prep_weight(body, w, n_out, tr, name):
    n_layers, K, N = w.shape
    return pl.pallas_call(
        body,
        grid=(n_layers, K // tr),
        in_specs=[pl.BlockSpec((None, tr, N), lambda l, i: (l, i, 0))],
        out_specs=pl.BlockSpec((None, tr, n_out), lambda l, i: (l, i, 0)),
        out_shape=jax.ShapeDtypeStruct((n_layers, K, n_out), BF16),
        compiler_params=_cparams(("parallel", "parallel")),
        name=name,
    )(w)


def _prep_layer(norm_g, gate_bias, pool_w, pool_scale, mlstm_norm_g, qlat_g, kvlat_g, w_ukv, qn_g, qr_g, kn_g, kr_g):
    wkv = w_ukv.reshape(KV_LORA, MLA_HEADS, MLA_NOPE + MLA_V)
    w_ukv_p = jnp.concatenate([wkv[:, :, :MLA_NOPE].reshape(KV_LORA, -1),
                               wkv[:, :, MLA_NOPE:].reshape(KV_LORA, -1)], axis=1).astype(BF16)
    return dict(
        norm_g=norm_g.reshape(1, D_MODEL),
        bias_misc=_pad_lanes(gate_bias, MISC_GATE0),
        pool_w=pool_w.astype(BF16), pool_scale=pool_scale.reshape(1, D_POOL),
        mlstm_norm_g=mlstm_norm_g.reshape(1, D_MLSTM),
        qlat_g=qlat_g.reshape(1, Q_LORA),
        kvlat_g=kvlat_g.reshape(1, KV_LORA), w_ukv=w_ukv_p,
        qn_g=qn_g.reshape(1, MLA_NOPE), qr_g=_pad_lanes(qr_g),
        kn_g=kn_g.reshape(1, MLA_NOPE), kr_g=_pad_lanes(kr_g),
    )


def _rope_tables(S):
    pos = jnp.arange(S, dtype=F32)
    inv_freq = ROPE_THETA ** (-(jnp.arange(0, MLA_ROPE, 2, dtype=F32) / MLA_ROPE))
    ang = pos[:, None] * inv_freq[None, :]
    cos, sin = jnp.cos(ang), jnp.sin(ang)
    pad = jnp.zeros((S, LANES - MLA_ROPE), F32)
    return jnp.concatenate([cos, cos, pad], axis=1), jnp.concatenate([-sin, sin, pad], axis=1)


ATTN_SLOTS = 2


def _tiles(S):
    t = lambda pref: min(pref, S)
    n_chunks = S // t(512)
    slots = min(ATTN_SLOTS, n_chunks)
    return dict(tm_in=t(512), tn_in=768, tm_pool=t(512), tm_mo=t(512), tm_q=t(512), q_heads=4,
                tk=t(512), tq=t(512), attn_group=min(16, n_chunks), attn_slots=slots, tm_out=t(512), tn_out=1024)


def _layer(x, layer, p, big, cos_t, sin_t):
    S = x.shape[0]
    t = _tiles(S)
    u = _in_proj(x, p["norm_g"], big["w_in"], layer, t["tm_in"], t["tn_in"])
    pool_o = _pool(u, p["pool_w"], p["pool_scale"], t["tm_pool"])
    hf, hb = _mlstm(u, p["bias_misc"])
    mlstm_o = _mlstm_out(hf, hb, u, p["mlstm_norm_g"], t["tm_mo"])
    qt = _q_up(u, p["qlat_g"], big["w_uq"], layer, p["qn_g"], p["qr_g"], cos_t, sin_t, t["tm_q"], t["q_heads"])
    kf, vt = _kv_up(u, p["kvlat_g"], p["w_ukv"], p["kn_g"], p["kr_g"], cos_t, sin_t, t["tk"])
    mla_o = _attn(qt, kf, vt, u, t["tq"], t["attn_group"], t["attn_slots"])
    return _out_proj(pool_o, mlstm_o, mla_o, big["w_out"], layer, x, t["tm_out"], t["tn_out"])


def _trunk(x, layers, big, cos_t, sin_t):
    x = x[0]
    for layer, p in enumerate(layers):
        x = _layer(x, layer, p, big, cos_t, sin_t)
    return x[None]


def kernel(x_prompt, x_sample, norm_g, w_in, gate_bias, pool_w, pool_scale, mlstm_norm_g,
           qlat_g, w_uq, kvlat_g, w_ukv, qn_g, qr_g, kn_g, kr_g, w_out):
    small = (norm_g, gate_bias, pool_w, pool_scale, mlstm_norm_g, qlat_g, kvlat_g, w_ukv, qn_g, qr_g, kn_g, kr_g)
    layers = [_prep_layer(*[w[l] for w in small]) for l in range(norm_g.shape[0])]
    big = dict(w_in=_prep_w_in(w_in),
               w_uq=_prep_weight(_w_uq_prep_kernel, w_uq, MLA_HEADS * QK_PAD, 512, "w_uq_prep"),
               w_out=w_out.astype(BF16))
    cos_t, sin_t = _rope_tables(max(x_prompt.shape[1], x_sample.shape[1]))
    return (_trunk(x_prompt, layers, big, cos_t, sin_t), _trunk(x_sample, layers, big, cos_t, sin_t))
```

```python
import functools

import numpy as np
import jax
import jax.numpy as jnp
from jax import lax
from jax.experimental import pallas as pl
from jax.experimental.pallas import tpu as pltpu

D_MODEL = 4096
D_POOL = 1024
D_MLSTM = 1024
D_MLA = 2048
POOL_WINDOWS = (2, 4, 8, 16)
POOL_GW = 256
MLSTM_HEADS = 4
MLSTM_HD = 256
MLSTM_CHUNK = 128
MLSTM_STEP_CHUNKS = 4
MLSTM_GATES = 16
MLA_HEADS = 16
MLA_V = 128
MLA_NOPE = 128
MLA_ROPE = 64
MLA_QK = 192
Q_LORA = 1536
KV_LORA = 512
ROPE_THETA = 10000.0
NORM_EPS = 1e-6

LANES = 128
QK_PAD = 256
V_ONES = 16
V_AUG = MLA_V + V_ONES
LOG2_E = 1.4426950408889634
VMEM_LIMIT = 56 * 1024 * 1024

OFF_QLAT = 0
OFF_KVLAT = 1536
OFF_AZ = 2048
OFF_PX = 4096
OFF_PZ = 5120
OFF_MQ = 6144
OFF_MK = 7168
OFF_MV = 8192
OFF_MO = 9216
OFF_MZ = 10240
OFF_MISC = 11264
N_IN_PAD = 11520
MISC_GATE0 = MLA_ROPE

NEG_BIG = -1e30

F32 = jnp.float32
BF16 = jnp.bfloat16


def _cparams(sem):
    return pltpu.CompilerParams(dimension_semantics=sem, vmem_limit_bytes=VMEM_LIMIT)


def _sigmoid(x):
    return 1.0 / (1.0 + jnp.exp(-x))


def _silu(x):
    return x * _sigmoid(x)


def _dot(a, b):
    return jnp.dot(a, b, preferred_element_type=F32)


def _dot_nt(a, b):
    return lax.dot_general(a, b, (((1,), (1,)), ((), ())), preferred_element_type=F32)


def _dot_tn(a, b):
    return lax.dot_general(a, b, (((0,), (0,)), ((), ())), preferred_element_type=F32)


def _norm_matmul_kernel(x_ref, g_ref, w_ref, o_ref, h_ref):
    @pl.when(pl.program_id(1) == 0)
    def _():
        x = x_ref[...]
        ms = jnp.mean(x * x, axis=-1, keepdims=True)
        h_ref[...] = (x * lax.rsqrt(ms + NORM_EPS) * g_ref[...]).astype(BF16)

    o_ref[...] = _dot(h_ref[...], w_ref[...]).astype(o_ref.dtype)


def _in_proj(x, g, w, layer, tm, tn):
    S, K = x.shape
    N = w.shape[2]
    return pl.pallas_call(
        _norm_matmul_kernel,
        grid=(S // tm, N // tn),
        in_specs=[
            pl.BlockSpec((tm, K), lambda i, j: (i, 0)),
            pl.BlockSpec((1, K), lambda i, j: (0, 0)),
            pl.BlockSpec((None, K, tn), lambda i, j: (layer, 0, j)),
        ],
        out_specs=pl.BlockSpec((tm, tn), lambda i, j: (i, j)),
        out_shape=jax.ShapeDtypeStruct((S, N), F32),
        scratch_shapes=[pltpu.VMEM((tm, K), BF16)],
        compiler_params=_cparams(("parallel", "arbitrary")),
        name="in_proj",
    )(x, g, w)


POOL_HALO = 8


def _pool_kernel(prev_ref, cur_ref, next_ref, z_ref, w_ref, scale_ref, o_ref, *, seq_len):
    i = pl.program_id(0)
    tm = cur_ref.shape[0]
    rows = tm + 2 * POOL_HALO
    prev = jnp.where(i > 0, prev_ref[...], 0.0)
    nxt = jnp.where(i < pl.num_programs(0) - 1, next_ref[...], 0.0)
    cur = cur_ref[...]
    xall = jnp.concatenate([prev, cur, nxt], axis=0)
    pos = i * tm + lax.broadcasted_iota(jnp.int32, (tm, 1), 0)
    for g, win in enumerate(POOL_WINDOWS):
        half = win // 2
        sl = slice(g * POOL_GW, (g + 1) * POOL_GW)
        acc = xall[:, sl]
        span = 1
        while span < win:
            acc = acc + pltpu.roll(acc, span, axis=0)
            span *= 2
        shift = half - 1
        if shift:
            acc = pltpu.roll(acc, rows - shift, axis=0)
        wsum = acc[POOL_HALO:POOL_HALO + tm]
        cnt = jnp.minimum(pos + half, seq_len) - jnp.maximum(pos - half, 0)
        diff = wsum / cnt.astype(F32) - cur[:, sl]
        y = _dot(diff.astype(BF16), w_ref[g])
        o_ref[:, sl] = (y * scale_ref[:, sl] * _silu(z_ref[:, sl])).astype(o_ref.dtype)


def _pool(u, pool_w, pool_scale, tm):
    S = u.shape[0]
    hb = tm // POOL_HALO
    n_halo = S // POOL_HALO
    cpx = OFF_PX // D_POOL
    return pl.pallas_call(
        functools.partial(_pool_kernel, seq_len=S),
        grid=(S // tm,),
        in_specs=[
            pl.BlockSpec((POOL_HALO, D_POOL), lambda i: (jnp.maximum(i * hb - 1, 0), cpx)),
            pl.BlockSpec((tm, D_POOL), lambda i: (i, cpx)),
            pl.BlockSpec((POOL_HALO, D_POOL), lambda i: (jnp.minimum((i + 1) * hb, n_halo - 1), cpx)),
            pl.BlockSpec((tm, D_POOL), lambda i: (i, OFF_PZ // D_POOL)),
            pl.BlockSpec((len(POOL_WINDOWS), POOL_GW, POOL_GW), lambda i: (0, 0, 0)),
            pl.BlockSpec((1, D_POOL), lambda i: (0, 0)),
        ],
        out_specs=pl.BlockSpec((tm, D_POOL), lambda i: (i, 0)),
        out_shape=jax.ShapeDtypeStruct((S, D_POOL), BF16),
        compiler_params=_cparams(("parallel",)),
        name="pool",
    )(u, u, u, u, pool_w, pool_scale)


def _split3(x):
    hi = x.astype(BF16)
    r1 = x - hi.astype(F32)
    mid = r1.astype(BF16)
    lo = (r1 - mid.astype(F32)).astype(BF16)
    return hi, mid, lo


def _log_sigmoid(x):
    return jnp.minimum(x, 0.0) - jnp.log(1.0 + jnp.exp(-jnp.abs(x)))


def _lane_tile(x, reps):
    return jnp.concatenate([x] * reps, axis=1)


def _mlstm_chain(q, k, v, i_rep, b_rep, pm_rep, i_row, b_row, c_old, m, reverse):
    L = q.shape[0]
    row = lax.broadcasted_iota(jnp.int32, (L, L), 0)
    col = lax.broadcasted_iota(jnp.int32, (L, L), 1)
    keep = (col >= row) if reverse else (col <= row)
    b_last = b_rep[0:1] if reverse else b_rep[L - 1:L]

    m_t = b_rep + jnp.maximum(m, pm_rep)
    qb = q.astype(BF16)
    ks = k * (MLSTM_HD ** -0.5)
    v_aug = jnp.concatenate([v.astype(BF16), jnp.ones((L, LANES), BF16)], axis=1)
    dexp = jnp.exp(jnp.where(keep, (b_rep - m_t) - (b_row - i_row), NEG_BIG))
    w = dexp * _dot_nt(qb, ks.astype(BF16))
    ei = jnp.exp(b_rep + m - m_t)
    lhs = jnp.concatenate([(_lane_tile(ei, MLSTM_HD // LANES) * q).astype(BF16), w.astype(BF16)], axis=1)
    numden = _dot(lhs, jnp.concatenate([c_old.astype(BF16), v_aug], axis=0))
    den = jnp.maximum(jnp.abs(numden[:, MLSTM_HD:]), jnp.exp(-m_t))
    h = numden[:, :MLSTM_HD] / _lane_tile(den, MLSTM_HD // LANES)

    a_rep = b_last - b_rep + i_rep
    m_new = jnp.maximum(b_last + m, jnp.max(a_rep, axis=0, keepdims=True))
    decay = jnp.exp(b_last + m - m_new)
    ek = _lane_tile(jnp.exp(a_rep - m_new), MLSTM_HD // LANES) * ks
    c_new = _lane_tile(decay, c_old.shape[1] // LANES) * c_old + _dot_tn(ek.astype(BF16), v_aug)
    return h, c_new, m_new


def _mlstm_gates(g, d, reverse):
    L = g.shape[0]
    row = lax.broadcasted_iota(jnp.int32, (L, L), 0)
    col = lax.broadcasted_iota(jnp.int32, (L, L), 1)
    lane_i0 = MISC_GATE0 + (2 * d) * MLSTM_HEADS
    tri = jnp.where((col >= row) if reverse else (col <= row), 1.0, 0.0).astype(BF16)
    b_all = sum(_dot(tri, p) for p in _split3(_log_sigmoid(g)))
    pm_all = g - pltpu.roll(b_all, LANES - MLSTM_HEADS, axis=1)
    shift = 1
    while shift < L:
        if reverse:
            moved = jnp.where(row < L - shift, pltpu.roll(pm_all, L - shift, axis=0), NEG_BIG)
        else:
            moved = jnp.where(row >= shift, pltpu.roll(pm_all, shift, axis=0), NEG_BIG)
        pm_all = jnp.maximum(pm_all, moved)
        shift *= 2
    gt = g.T
    bt = b_all.T
    lane_rep = lambda x, lane: jnp.broadcast_to(x[:, lane:lane + 1], (L, LANES))
    heads = []
    for hd in range(MLSTM_HEADS):
        lane_i = lane_i0 + hd
        lane_f = lane_i0 + MLSTM_HEADS + hd
        heads.append((lane_rep(g, lane_i), lane_rep(b_all, lane_f), lane_rep(pm_all, lane_i),
                      gt[lane_i:lane_i + 1, :], bt[lane_f:lane_f + 1, :]))
    return heads


def _mlstm_kernel(qf_ref, kf_ref, vf_ref, gf_ref, qb_ref, kb_ref, vb_ref, gb_ref, bias_ref,
                  hf_ref, hb_ref, c_ref, m_ref):
    @pl.when(pl.program_id(0) == 0)
    def _():
        c_ref[...] = jnp.zeros_like(c_ref)
        m_ref[...] = jnp.zeros_like(m_ref)

    L = MLSTM_CHUNK
    n_sub = gf_ref.shape[0] // L
    dirs = ((qf_ref, kf_ref, vf_ref, gf_ref, hf_ref, False), (qb_ref, kb_ref, vb_ref, gb_ref, hb_ref, True))
    scan_order = lambda reverse: range(n_sub - 1, -1, -1) if reverse else range(n_sub)
    gates = {(d, sub): _mlstm_gates(g_ref[sub * L:(sub + 1) * L, :] + bias_ref[...], d, reverse)
             for d, (_, _, _, g_ref, _, reverse) in enumerate(dirs) for sub in scan_order(reverse)}
    outputs, states = [], []
    for d, (q_ref, k_ref, v_ref, g_ref, h_ref, reverse) in enumerate(dirs):
        for hd in range(MLSTM_HEADS):
            sl = slice(hd * MLSTM_HD, (hd + 1) * MLSTM_HD)
            idx = d * MLSTM_HEADS + hd
            c, m = c_ref[idx], m_ref[idx]
            for sub in scan_order(reverse):
                rows = slice(sub * L, (sub + 1) * L)
                h, c, m = _mlstm_chain(q_ref[rows, sl], k_ref[rows, sl], v_ref[rows, sl], *gates[d, sub][hd],
                                       c, m, reverse)
                outputs.append((h_ref, rows, sl, h))
            states.append((idx, c, m))
    for h_ref, rows, sl, h in outputs:
        h_ref[rows, sl] = h
    for idx, c, m in states:
        c_ref[idx] = c
        m_ref[idx] = m


def _mlstm(u, bias_misc):
    S = u.shape[0]
    L = MLSTM_STEP_CHUNKS * MLSTM_CHUNK
    nc = S // L
    fwd = lambda c: c
    bwd = lambda c: nc - 1 - c

    def spec(off, width, cmap):
        return pl.BlockSpec((L, width), lambda c: (cmap(c), off // width))

    in_specs = []
    for cmap in (fwd, bwd):
        in_specs += [spec(OFF_MQ, D_MLSTM, cmap), spec(OFF_MK, D_MLSTM, cmap), spec(OFF_MV, D_MLSTM, cmap),
                     spec(OFF_MISC, LANES, cmap)]
    in_specs.append(pl.BlockSpec((1, LANES), lambda c: (0, 0)))
    nstate = 2 * MLSTM_HEADS
    return pl.pallas_call(
        _mlstm_kernel,
        grid=(nc,),
        in_specs=in_specs,
        out_specs=[pl.BlockSpec((L, D_MLSTM), lambda c: (c, 0)),
                   pl.BlockSpec((L, D_MLSTM), lambda c: (nc - 1 - c, 0))],
        out_shape=[jax.ShapeDtypeStruct((S, D_MLSTM), F32)] * 2,
        scratch_shapes=[pltpu.VMEM((nstate, MLSTM_HD, MLSTM_HD + LANES), F32),
                        pltpu.VMEM((nstate, 1, LANES), F32)],
        compiler_params=_cparams(("arbitrary",)),
        name="mlstm",
    )(u, u, u, u, u, u, u, u, bias_misc)


def _mlstm_out_kernel(hf_ref, hb_ref, o_ref, z_ref, g_ref, out_ref):
    for hd in range(MLSTM_HEADS):
        sl = slice(hd * MLSTM_HD, (hd + 1) * MLSTM_HD)
        h = hf_ref[:, sl] + hb_ref[:, sl]
        ms = jnp.mean(h * h, axis=-1, keepdims=True)
        hn = h * lax.rsqrt(ms + NORM_EPS) * g_ref[:, sl]
        out_ref[:, sl] = (_sigmoid(o_ref[:, sl]) * hn * _silu(z_ref[:, sl])).astype(out_ref.dtype)


def _mlstm_out(hf, hb, u, norm_g, tm):
    S = hf.shape[0]
    return pl.pallas_call(
        _mlstm_out_kernel,
        grid=(S // tm,),
        in_specs=[
            pl.BlockSpec((tm, D_MLSTM), lambda i: (i, 0)),
            pl.BlockSpec((tm, D_MLSTM), lambda i: (i, 0)),
            pl.BlockSpec((tm, D_MLSTM), lambda i: (i, OFF_MO // D_MLSTM)),
            pl.BlockSpec((tm, D_MLSTM), lambda i: (i, OFF_MZ // D_MLSTM)),
            pl.BlockSpec((1, D_MLSTM), lambda i: (0, 0)),
        ],
        out_specs=pl.BlockSpec((tm, D_MLSTM), lambda i: (i, 0)),
        out_shape=jax.ShapeDtypeStruct((S, D_MLSTM), BF16),
        compiler_params=_cparams(("parallel",)),
        name="mlstm_out",
    )(hf, hb, u, u, norm_g)


def _rope128(x, g, cos_ref, sin_ref):
    lane = lax.broadcasted_iota(jnp.int32, x.shape, 1)
    x = jnp.where(lane < MLA_ROPE, x, 0.0)
    ms = jnp.sum(x * x, axis=-1, keepdims=True) * (1.0 / MLA_ROPE)
    xn = x * lax.rsqrt(ms + NORM_EPS) * g
    half = MLA_ROPE // 2
    partner = jnp.where(lane < half, pltpu.roll(xn, LANES - half, axis=1), pltpu.roll(xn, half, axis=1))
    return xn * cos_ref[...] + partner * sin_ref[...]


def _q_up_kernel(x_ref, g_ref, w_ref, qn_ref, qr_ref, cos_ref, sin_ref, o_ref, h_ref, acc_ref,
                 *, heads_per_tile, n_col, n_tiles):
    s = pl.program_id(0)

    @pl.when(jnp.logical_and(s % n_col == 0, s < n_tiles))
    def _():
        x = x_ref[...]
        ms = jnp.mean(x * x, axis=-1, keepdims=True)
        h_ref[...] = (x * lax.rsqrt(ms + NORM_EPS) * g_ref[...]).astype(BF16)

    @pl.when(s == 0)
    def _():
        acc_ref[1] = jnp.zeros(acc_ref.shape[1:], acc_ref.dtype)

    scale = MLA_QK ** -0.5 * LOG2_E

    def step(fill, drain):
        for hh in range(heads_per_tile):
            cols = slice(hh * QK_PAD, (hh + 1) * QK_PAD)
            acc_ref[fill, :, cols] = _dot(h_ref[...], w_ref[:, cols])
        for hh in range(heads_per_tile):
            nope = acc_ref[drain, :, hh * QK_PAD: hh * QK_PAD + MLA_NOPE]
            ms = jnp.mean(nope * nope, axis=-1, keepdims=True)
            nope = nope * lax.rsqrt(ms + NORM_EPS) * qn_ref[...]
            pe = _rope128(acc_ref[drain, :, hh * QK_PAD + MLA_NOPE: (hh + 1) * QK_PAD], qr_ref[...], cos_ref, sin_ref)
            o_ref[hh * QK_PAD: hh * QK_PAD + MLA_NOPE, :] = (nope * scale).astype(o_ref.dtype).T
            o_ref[hh * QK_PAD + MLA_NOPE: (hh + 1) * QK_PAD, :] = (pe * scale).astype(o_ref.dtype).T

    @pl.when(s % 2 == 0)
    def _():
        step(0, 1)

    @pl.when(s % 2 == 1)
    def _():
        step(1, 0)


def _q_up(u, g, w, layer, qn_g, qr_g, cos_t, sin_t, tm, heads_per_tile):
    S = u.shape[0]
    tn = heads_per_tile * QK_PAD
    N = MLA_HEADS * QK_PAD
    n_col = N // tn
    n_tiles = (S // tm) * n_col
    fill_row = lambda s: jnp.minimum(s, n_tiles - 1) // n_col
    fill_col = lambda s: jnp.minimum(s, n_tiles - 1) % n_col
    drain_row = lambda s: jnp.maximum(s - 1, 0) // n_col
    drain_col = lambda s: jnp.maximum(s - 1, 0) % n_col
    return pl.pallas_call(
        functools.partial(_q_up_kernel, heads_per_tile=heads_per_tile, n_col=n_col, n_tiles=n_tiles),
        grid=(n_tiles + 1,),
        in_specs=[
            pl.BlockSpec((tm, Q_LORA), lambda s: (fill_row(s), OFF_QLAT // Q_LORA)),
            pl.BlockSpec((1, Q_LORA), lambda s: (0, 0)),
            pl.BlockSpec((None, Q_LORA, tn), lambda s: (layer, 0, fill_col(s))),
            pl.BlockSpec((1, LANES), lambda s: (0, 0)),
            pl.BlockSpec((1, LANES), lambda s: (0, 0)),
            pl.BlockSpec((tm, LANES), lambda s: (drain_row(s), 0)),
            pl.BlockSpec((tm, LANES), lambda s: (drain_row(s), 0)),
        ],
        out_specs=pl.BlockSpec((tn, tm), lambda s: (drain_col(s), drain_row(s))),
        out_shape=jax.ShapeDtypeStruct((N, S), BF16),
        scratch_shapes=[pltpu.VMEM((tm, Q_LORA), BF16), pltpu.VMEM((2, tm, tn), F32)],
        compiler_params=_cparams(("arbitrary",)),
        name="q_up",
    )(u, g, w, qn_g, qr_g, cos_t, sin_t)


def _kv_up_kernel(x_ref, misc_ref, g_ref, w_ref, kn_ref, kr_ref, cos_ref, sin_ref, k_ref, vt_ref):
    x = x_ref[...]
    ms = jnp.mean(x * x, axis=-1, keepdims=True)
    h = (x * lax.rsqrt(ms + NORM_EPS) * g_ref[...]).astype(BF16)
    acc = _dot(h, w_ref[...])
    pe = _rope128(misc_ref[...], kr_ref[...], cos_ref, sin_ref).astype(k_ref.dtype)
    for hd in range(MLA_HEADS):
        nope = acc[:, hd * MLA_NOPE: (hd + 1) * MLA_NOPE]
        ms = jnp.mean(nope * nope, axis=-1, keepdims=True)
        k_ref[:, hd * QK_PAD: hd * QK_PAD + MLA_NOPE] = (nope * lax.rsqrt(ms + NORM_EPS) * kn_ref[...]).astype(k_ref.dtype)
        k_ref[:, hd * QK_PAD + MLA_NOPE: (hd + 1) * QK_PAD] = pe
        v = acc[:, MLA_HEADS * MLA_NOPE + hd * MLA_V: MLA_HEADS * MLA_NOPE + (hd + 1) * MLA_V]
        vt_ref[hd * V_AUG: hd * V_AUG + MLA_V, :] = v.astype(vt_ref.dtype).T
        vt_ref[hd * V_AUG + MLA_V: (hd + 1) * V_AUG, :] = jnp.ones((V_ONES, vt_ref.shape[1]), vt_ref.dtype)


def _kv_up(u, g, w, kn_g, kr_g, cos_t, sin_t, tk):
    S = u.shape[0]
    return pl.pallas_call(
        _kv_up_kernel,
        grid=(S // tk,),
        in_specs=[
            pl.BlockSpec((tk, KV_LORA), lambda i: (i, OFF_KVLAT // KV_LORA)),
            pl.BlockSpec((tk, LANES), lambda i: (i, OFF_MISC // LANES)),
            pl.BlockSpec((1, KV_LORA), lambda i: (0, 0)),
            pl.BlockSpec((KV_LORA, 2 * D_MLA), lambda i: (0, 0)),
            pl.BlockSpec((1, LANES), lambda i: (0, 0)),
            pl.BlockSpec((1, LANES), lambda i: (0, 0)),
            pl.BlockSpec((tk, LANES), lambda i: (i, 0)),
            pl.BlockSpec((tk, LANES), lambda i: (i, 0)),
        ],
        out_specs=[pl.BlockSpec((tk, MLA_HEADS * QK_PAD), lambda i: (i, 0)),
                   pl.BlockSpec((None, MLA_HEADS * V_AUG, tk), lambda i: (i, 0, 0))],
        out_shape=[jax.ShapeDtypeStruct((S, MLA_HEADS * QK_PAD), BF16),
                   jax.ShapeDtypeStruct((S // tk, MLA_HEADS * V_AUG, tk), BF16)],
        compiler_params=_cparams(("parallel",)),
        name="kv_up",
    )(u, u, g, w, kn_g, kr_g, cos_t, sin_t)


def _attn_kernel(qt_ref, qnext_ref, k_ref, vt_ref, z_ref, o_ref, s_ref, acc_ref, cmax_ref, fin_ref,
                 *, group_size, n_q, n_tiles):
    n_chunks, _, tk = vt_ref.shape
    tq = qt_ref.shape[1]
    n_slots = s_ref.shape[0]
    step = pl.program_id(0)
    q_tile = jnp.minimum(step, n_tiles - 1) % n_q

    def scores(q_ref, j, slot):
        k = k_ref[pl.ds(pl.multiple_of(j * tk, tk), tk), :]
        st = _dot(k, q_ref[...])
        s_ref[slot] = st
        return jnp.max(st, axis=0, keepdims=True)

    def accumulate(j, slot, m, cmax):
        m_new = jnp.maximum(m, cmax)
        alpha = jnp.exp2(m - m_new)
        p = jnp.exp2(s_ref[slot] - m_new).astype(BF16)
        acc_ref[...] = alpha * acc_ref[...] + _dot(vt_ref[j], p)
        return m_new

    def group(j0, m, cmax, last):
        for u in range(group_size):
            if last and u == group_size - 1:
                cnext = scores(qnext_ref, 0, 0)
            else:
                cnext = scores(qt_ref, j0 + u + 1, (u + 1) % n_slots)
            m = accumulate(j0 + u, u % n_slots, m, cmax)
            cmax = cnext
        return m, cmax

    @pl.when(step == 0)
    def _():
        fin_ref[...] = jnp.ones_like(fin_ref)

    @pl.when(q_tile == 0)
    def _():
        cmax_ref[...] = scores(qt_ref, 0, 0)

    o = (fin_ref[0:MLA_V, :] / fin_ref[MLA_V:MLA_V + 1, :]).T
    o_ref[...] = (o * _silu(z_ref[...])).astype(o_ref.dtype)

    acc_ref[...] = jnp.zeros_like(acc_ref)
    m = jnp.full((1, tq), NEG_BIG, F32)
    n_groups = n_chunks // group_size
    m, cmax = lax.fori_loop(0, n_groups - 1, lambda g, c: group(g * group_size, c[0], c[1], False),
                            (m, cmax_ref[...]))
    _, cmax_ref[...] = group((n_groups - 1) * group_size, m, cmax, True)
    fin_ref[...] = acc_ref[...]


def _attn(qt, kf, vt, u, tq, group_size, n_slots):
    S = kf.shape[0]
    n_chunks, _, tk = vt.shape
    n_q = S // tq
    n_tiles = MLA_HEADS * n_q
    assert n_chunks % group_size == 0 and group_size % n_slots == 0 and n_slots >= 2
    tile = lambda s: jnp.minimum(s, n_tiles - 1)
    done = lambda s: jnp.maximum(s - 1, 0)
    return pl.pallas_call(
        functools.partial(_attn_kernel, group_size=group_size, n_q=n_q, n_tiles=n_tiles),
        grid=(n_tiles + 1,),
        in_specs=[
            pl.BlockSpec((QK_PAD, tq), lambda s: (tile(s) // n_q, tile(s) % n_q)),
            pl.BlockSpec((QK_PAD, tq), lambda s: (tile(s) // n_q, jnp.minimum(tile(s) % n_q + 1, n_q - 1))),
            pl.BlockSpec((S, QK_PAD), lambda s: (0, tile(s) // n_q)),
            pl.BlockSpec((n_chunks, V_AUG, tk), lambda s: (0, tile(s) // n_q, 0)),
            pl.BlockSpec((tq, MLA_V), lambda s: (done(s) % n_q, OFF_AZ // MLA_V + done(s) // n_q)),
        ],
        out_specs=pl.BlockSpec((tq, MLA_V), lambda s: (done(s) % n_q, done(s) // n_q)),
        out_shape=jax.ShapeDtypeStruct((S, D_MLA), BF16),
        scratch_shapes=[pltpu.VMEM((n_slots, tk, tq), F32), pltpu.VMEM((V_AUG, tq), F32),
                        pltpu.VMEM((1, tq), F32), pltpu.VMEM((V_AUG, tq), F32)],
        compiler_params=_cparams(("arbitrary",)),
        name="attn",
    )(qt, qt, kf, vt, u)


def _out_proj_kernel(p_ref, m_ref, a_ref, w_ref, x_ref, o_ref):
    acc = _dot(p_ref[...], w_ref[0:D_POOL, :])
    acc += _dot(m_ref[...], w_ref[D_POOL:D_POOL + D_MLSTM, :])
    acc += _dot(a_ref[...], w_ref[D_POOL + D_MLSTM:, :])
    o_ref[...] = x_ref[...] + acc


def _out_proj(pool_o, mlstm_o, mla_o, w, layer, x, tm, tn):
    S = x.shape[0]
    return pl.pallas_call(
        _out_proj_kernel,
        grid=(S // tm, D_MODEL // tn),
        in_specs=[
            pl.BlockSpec((tm, D_POOL), lambda i, j: (i, 0)),
            pl.BlockSpec((tm, D_MLSTM), lambda i, j: (i, 0)),
            pl.BlockSpec((tm, D_MLA), lambda i, j: (i, 0)),
            pl.BlockSpec((None, D_MODEL, tn), lambda i, j: (layer, 0, j)),
            pl.BlockSpec((tm, tn), lambda i, j: (i, j)),
        ],
        out_specs=pl.BlockSpec((tm, tn), lambda i, j: (i, j)),
        out_shape=jax.ShapeDtypeStruct((S, D_MODEL), F32),
        compiler_params=_cparams(("parallel", "arbitrary")),
        name="out_proj",
    )(pool_o, mlstm_o, mla_o, w, x)


def _pad_lanes(v, offset=0):
    return jnp.zeros((1, LANES), F32).at[0, offset:offset + v.shape[0]].set(v.astype(F32))


_SRC_SIZES = (D_POOL, D_POOL, D_MLSTM, D_MLSTM, D_MLSTM, D_MLSTM, D_MLSTM, MLSTM_GATES,
              Q_LORA, KV_LORA, MLA_ROPE, D_MLA)
(_SRC_PX, _, _, _, _, _, _, _SRC_MG, _SRC_QLAT, _SRC_KVLAT, _SRC_KROPE, _SRC_AZ) = (
    int(o) for o in np.concatenate([[0], np.cumsum(_SRC_SIZES)])[:-1])
W_IN_MOVES = (
    (OFF_QLAT, _SRC_QLAT, Q_LORA), (OFF_KVLAT, _SRC_KVLAT, KV_LORA), (OFF_AZ, _SRC_AZ, D_MLA),
    (OFF_PX, _SRC_PX, 2 * D_POOL + 5 * D_MLSTM),
)
W_IN_TILE = 256
W_IN_MISC_TILE = OFF_MISC // W_IN_TILE


def _w_in_src_row(j):
    unit = 16
    row = jnp.int32(_SRC_KROPE // unit)
    for dst, src, width in reversed(W_IN_MOVES):
        row = jnp.where(j < (dst + width) // W_IN_TILE, src // unit + (j - dst // W_IN_TILE) * (W_IN_TILE // unit), row)
    return row * unit


def _w_in_prep_kernel(rows_ref, gate_rows_ref, o_ref):
    j = pl.program_id(1)

    @pl.when(j != W_IN_MISC_TILE)
    def _():
        o_ref[...] = rows_ref[0].T.astype(o_ref.dtype)

    @pl.when(j == W_IN_MISC_TILE)
    def _():
        pad = jnp.zeros((W_IN_TILE - MLA_ROPE - MLSTM_GATES, rows_ref.shape[2]), rows_ref.dtype)
        tile = jnp.concatenate([rows_ref[0, 0:MLA_ROPE, :], gate_rows_ref[0, 0:MLSTM_GATES, :], pad], axis=0)
        o_ref[...] = tile.T.astype(o_ref.dtype)


def _prep_w_in(w_in):
    wt = jnp.swapaxes(w_in, 1, 2)
    n_layers, _, K = wt.shape
    elem = pl.Element
    return pl.pallas_call(
        _w_in_prep_kernel,
        grid=(n_layers, N_IN_PAD // W_IN_TILE),
        in_specs=[pl.BlockSpec((elem(1), elem(W_IN_TILE), elem(K)), lambda l, j: (l, _w_in_src_row(j), 0)),
                  pl.BlockSpec((elem(1), elem(W_IN_TILE), elem(K)), lambda l, j: (l, _SRC_MG, 0))],
        out_specs=pl.BlockSpec((None, K, W_IN_TILE), lambda l, j: (l, 0, j)),
        out_shape=jax.ShapeDtypeStruct((n_layers, K, N_IN_PAD), BF16),
        compiler_params=_cparams(("parallel", "parallel")),
        name="w_in_prep",
    )(wt, wt)


def _w_uq_prep_kernel(w_ref, o_ref):
    for hd in range(MLA_HEADS):
        o_ref[:, hd * QK_PAD: hd * QK_PAD + MLA_QK] = w_ref[:, hd * MLA_QK: (hd + 1) * MLA_QK].astype(o_ref.dtype)
        o_ref[:, hd * QK_PAD + MLA_QK: (hd + 1) * QK_PAD] = jnp.zeros((o_ref.shape[0], QK_PAD - MLA_QK), o_ref.dtype)


def _prep_weight(body, w, n_out, tr, name):
    n_layers, K, N = w.shape
    return pl.pallas_call(
        body,
        grid=(n_layers, K // tr),
        in_specs=[pl.BlockSpec((None, tr, N), lambda l, i: (l, i, 0))],
        out_specs=pl.BlockSpec((None, tr, n_out), lambda l, i: (l, i, 0)),
        out_shape=jax.ShapeDtypeStruct((n_layers, K, n_out), BF16),
        compiler_params=_cparams(("parallel", "parallel")),
        name=name,
    )(w)


def _prep_layer(norm_g, gate_bias, pool_w, pool_scale, mlstm_norm_g, qlat_g, kvlat_g, w_ukv, qn_g, qr_g, kn_g, kr_g):
    wkv = w_ukv.reshape(KV_LORA, MLA_HEADS, MLA_NOPE + MLA_V)
    w_ukv_p = jnp.concatenate([wkv[:, :, :MLA_NOPE].reshape(KV_LORA, -1),
                               wkv[:, :, MLA_NOPE:].reshape(KV_LORA, -1)], axis=1).astype(BF16)
    return dict(
        norm_g=norm_g.reshape(1, D_MODEL),
        bias_misc=_pad_lanes(gate_bias, MISC_GATE0),
        pool_w=pool_w.astype(BF16), pool_scale=pool_scale.reshape(1, D_POOL),
        mlstm_norm_g=mlstm_norm_g.reshape(1, D_MLSTM),
        qlat_g=qlat_g.reshape(1, Q_LORA),
        kvlat_g=kvlat_g.reshape(1, KV_LORA), w_ukv=w_ukv_p,
        qn_g=qn_g.reshape(1, MLA_NOPE), qr_g=_pad_lanes(qr_g),
        kn_g=kn_g.reshape(1, MLA_NOPE), kr_g=_pad_lanes(kr_g),
    )


def _rope_tables(S):
    pos = jnp.arange(S, dtype=F32)
    inv_freq = ROPE_THETA ** (-(jnp.arange(0, MLA_ROPE, 2, dtype=F32) / MLA_ROPE))
    ang = pos[:, None] * inv_freq[None, :]
    cos, sin = jnp.cos(ang), jnp.sin(ang)
    pad = jnp.zeros((S, LANES - MLA_ROPE), F32)
    return jnp.concatenate([cos, cos, pad], axis=1), jnp.concatenate([-sin, sin, pad], axis=1)


ATTN_SLOTS = 2


def _tiles(S):
    t = lambda pref: min(pref, S)
    n_chunks = S // t(512)
    slots = min(ATTN_SLOTS, n_chunks)
    return dict(tm_in=t(512), tn_in=768, tm_pool=t(512), tm_mo=t(512), tm_q=t(512), q_heads=4,
                tk=t(512), tq=t(512), attn_group=min(16, n_chunks), attn_slots=slots, tm_out=t(512), tn_out=1024)


def _layer(x, layer, p, big, cos_t, sin_t):
    S = x.shape[0]
    t = _tiles(S)
    u = _in_proj(x, p["norm_g"], big["w_in"], layer, t["tm_in"], t["tn_in"])
    pool_o = _pool(u, p["pool_w"], p["pool_scale"], t["tm_pool"])
    hf, hb = _mlstm(u, p["bias_misc"])
    mlstm_o = _mlstm_out(hf, hb, u, p["mlstm_norm_g"], t["tm_mo"])
    qt = _q_up(u, p["qlat_g"], big["w_uq"], layer, p["qn_g"], p["qr_g"], cos_t, sin_t, t["tm_q"], t["q_heads"])
    kf, vt = _kv_up(u, p["kvlat_g"], p["w_ukv"], p["kn_g"], p["kr_g"], cos_t, sin_t, t["tk"])
    mla_o = _attn(qt, kf, vt, u, t["tq"], t["attn_group"], t["attn_slots"])
    return _out_proj(pool_o, mlstm_o, mla_o, big["w_out"], layer, x, t["tm_out"], t["tn_out"])


def _trunk(x, layers, big, cos_t, sin_t):
    x = x[0]
    for layer, p in enumerate(layers):
        x = _layer(x, layer, p, big, cos_t, sin_t)
    return x[None]


def kernel(x_prompt, x_sample, norm_g, w_in, gate_bias, pool_w, pool_scale, mlstm_norm_g,
           qlat_g, w_uq, kvlat_g, w_ukv, qn_g, qr_g, kn_g, kr_g, w_out):
    small = (norm_g, gate_bias, pool_w, pool_scale, mlstm_norm_g, qlat_g, kvlat_g, w_ukv, qn_g, qr_g, kn_g, kr_g)
    layers = [_prep_layer(*[w[l] for w in small]) for l in range(norm_g.shape[0])]
    big = dict(w_in=_prep_w_in(w_in),
               w_uq=_prep_weight(_w_uq_prep_kernel, w_uq, MLA_HEADS * QK_PAD, 512, "w_uq_prep"),
               w_out=w_out.astype(BF16))
    cos_t, sin_t = _rope_tables(max(x_prompt.shape[1], x_sample.shape[1]))
    return (_trunk(x_prompt, layers, big, cos_t, sin_t), _trunk(x_sample, layers, big, cos_t, sin_t))
```

```python
import functools

import numpy as np
import jax
import jax.numpy as jnp
from jax import lax
from jax.experimental import pallas as pl
from jax.experimental.pallas import tpu as pltpu

D_MODEL = 4096
D_POOL = 1024
D_MLSTM = 1024
D_MLA = 2048
POOL_WINDOWS = (2, 4, 8, 16)
POOL_GW = 256
MLSTM_HEADS = 4
MLSTM_HD = 256
MLSTM_CHUNK = 128
MLSTM_STEP_CHUNKS = 4
MLSTM_GATES = 16
MLA_HEADS = 16
MLA_V = 128
MLA_NOPE = 128
MLA_ROPE = 64
MLA_QK = 192
Q_LORA = 1536
KV_LORA = 512
ROPE_THETA = 10000.0
NORM_EPS = 1e-6

LANES = 128
QK_PAD = 256
V_ONES = 16
V_AUG = MLA_V + V_ONES
LOG2_E = 1.4426950408889634
VMEM_LIMIT = 56 * 1024 * 1024

OFF_QLAT = 0
OFF_KVLAT = 1536
OFF_AZ = 2048
OFF_PX = 4096
OFF_PZ = 5120
OFF_MQ = 6144
OFF_MK = 7168
OFF_MV = 8192
OFF_MO = 9216
OFF_MZ = 10240
OFF_MISC = 11264
N_IN_PAD = 11520
MISC_GATE0 = MLA_ROPE

NEG_BIG = -1e30

F32 = jnp.float32
BF16 = jnp.bfloat16


def _cparams(sem):
    return pltpu.CompilerParams(dimension_semantics=sem, vmem_limit_bytes=VMEM_LIMIT)


def _sigmoid(x):
    return 1.0 / (1.0 + jnp.exp(-x))


def _silu(x):
    return x * _sigmoid(x)


def _dot(a, b):
    return jnp.dot(a, b, preferred_element_type=F32)


def _dot_nt(a, b):
    return lax.dot_general(a, b, (((1,), (1,)), ((), ())), preferred_element_type=F32)


def _dot_tn(a, b):
    return lax.dot_general(a, b, (((0,), (0,)), ((), ())), preferred_element_type=F32)


def _norm_matmul_kernel(x_ref, g_ref, w_ref, o_ref, h_ref):
    @pl.when(pl.program_id(1) == 0)
    def _():
        x = x_ref[...]
        ms = jnp.mean(x * x, axis=-1, keepdims=True)
        h_ref[...] = (x * lax.rsqrt(ms + NORM_EPS) * g_ref[...]).astype(BF16)

    o_ref[...] = _dot(h_ref[...], w_ref[...]).astype(o_ref.dtype)


def _in_proj(x, g, w, layer, tm, tn):
    S, K = x.shape
    N = w.shape[2]
    return pl.pallas_call(
        _norm_matmul_kernel,
        grid=(S // tm, N // tn),
        in_specs=[
            pl.BlockSpec((tm, K), lambda i, j: (i, 0)),
            pl.BlockSpec((1, K), lambda i, j: (0, 0)),
            pl.BlockSpec((None, K, tn), lambda i, j: (layer, 0, j)),
        ],
        out_specs=pl.BlockSpec((tm, tn), lambda i, j: (i, j)),
        out_shape=jax.ShapeDtypeStruct((S, N), F32),
        scratch_shapes=[pltpu.VMEM((tm, K), BF16)],
        compiler_params=_cparams(("parallel", "arbitrary")),
        name="in_proj",
    )(x, g, w)


POOL_HALO = 8


def _pool_kernel(prev_ref, cur_ref, next_ref, z_ref, w_ref, scale_ref, o_ref, *, seq_len):
    i = pl.program_id(0)
    tm = cur_ref.shape[0]
    rows = tm + 2 * POOL_HALO
    prev = jnp.where(i > 0, prev_ref[...], 0.0)
    nxt = jnp.where(i < pl.num_programs(0) - 1, next_ref[...], 0.0)
    cur = cur_ref[...]
    xall = jnp.concatenate([prev, cur, nxt], axis=0)
    pos = i * tm + lax.broadcasted_iota(jnp.int32, (tm, 1), 0)
    for g, win in enumerate(POOL_WINDOWS):
        half = win // 2
        sl = slice(g * POOL_GW, (g + 1) * POOL_GW)
        acc = xall[:, sl]
        span = 1
        while span < win:
            acc = acc + pltpu.roll(acc, span, axis=0)
            span *= 2
        shift = half - 1
        if shift:
            acc = pltpu.roll(acc, rows - shift, axis=0)
        wsum = acc[POOL_HALO:POOL_HALO + tm]
        cnt = jnp.minimum(pos + half, seq_len) - jnp.maximum(pos - half, 0)
        diff = wsum / cnt.astype(F32) - cur[:, sl]
        y = _dot(diff.astype(BF16), w_ref[g])
        o_ref[:, sl] = (y * scale_ref[:, sl] * _silu(z_ref[:, sl])).astype(o_ref.dtype)


def _pool(u, pool_w, pool_scale, tm):
    S = u.shape[0]
    hb = tm // POOL_HALO
    n_halo = S // POOL_HALO
    cpx = OFF_PX // D_POOL
    return pl.pallas_call(
        functools.partial(_pool_kernel, seq_len=S),
        grid=(S // tm,),
        in_specs=[
            pl.BlockSpec((POOL_HALO, D_POOL), lambda i: (jnp.maximum(i * hb - 1, 0), cpx)),
            pl.BlockSpec((tm, D_POOL), lambda i: (i, cpx)),
            pl.BlockSpec((POOL_HALO, D_POOL), lambda i: (jnp.minimum((i + 1) * hb, n_halo - 1), cpx)),
            pl.BlockSpec((tm, D_POOL), lambda i: (i, OFF_PZ // D_POOL)),
            pl.BlockSpec((len(POOL_WINDOWS), POOL_GW, POOL_GW), lambda i: (0, 0, 0)),
            pl.BlockSpec((1, D_POOL), lambda i: (0, 0)),
        ],
        out_specs=pl.BlockSpec((tm, D_POOL), lambda i: (i, 0)),
        out_shape=jax.ShapeDtypeStruct((S, D_POOL), BF16),
        compiler_params=_cparams(("parallel",)),
        name="pool",
    )(u, u, u, u, pool_w, pool_scale)


def _split3(x):
    hi = x.astype(BF16)
    r1 = x - hi.astype(F32)
    mid = r1.astype(BF16)
    lo = (r1 - mid.astype(F32)).astype(BF16)
    return hi, mid, lo


def _log_sigmoid(x):
    return jnp.minimum(x, 0.0) - jnp.log(1.0 + jnp.exp(-jnp.abs(x)))


def _lane_tile(x, reps):
    return jnp.concatenate([x] * reps, axis=1)


def _mlstm_chain(q, k, v, i_rep, b_rep, pm_rep, i_row, b_row, c_old, m, reverse):
    L = q.shape[0]
    row = lax.broadcasted_iota(jnp.int32, (L, L), 0)
    col = lax.broadcasted_iota(jnp.int32, (L, L), 1)
    keep = (col >= row) if reverse else (col <= row)
    b_last = b_rep[0:1] if reverse else b_rep[L - 1:L]

    m_t = b_rep + jnp.maximum(m, pm_rep)
    qb = q.astype(BF16)
    ks = k * (MLSTM_HD ** -0.5)
    v_aug = jnp.concatenate([v.astype(BF16), jnp.ones((L, LANES), BF16)], axis=1)
    dexp = jnp.exp(jnp.where(keep, (b_rep - m_t) - (b_row - i_row), NEG_BIG))
    w = dexp * _dot_nt(qb, ks.astype(BF16))
    ei = jnp.exp(b_rep + m - m_t)
    lhs = jnp.concatenate([(_lane_tile(ei, MLSTM_HD // LANES) * q).astype(BF16), w.astype(BF16)], axis=1)
    numden = _dot(lhs, jnp.concatenate([c_old.astype(BF16), v_aug], axis=0))
    den = jnp.maximum(jnp.abs(numden[:, MLSTM_HD:]), jnp.exp(-m_t))
    h = numden[:, :MLSTM_HD] / _lane_tile(den, MLSTM_HD // LANES)

    a_rep = b_last - b_rep + i_rep
    m_new = jnp.maximum(b_last + m, jnp.max(a_rep, axis=0, keepdims=True))
    decay = jnp.exp(b_last + m - m_new)
    ek = _lane_tile(jnp.exp(a_rep - m_new), MLSTM_HD // LANES) * ks
    c_new = _lane_tile(decay, c_old.shape[1] // LANES) * c_old + _dot_tn(ek.astype(BF16), v_aug)
    return h, c_new, m_new


def _mlstm_gates(g, d, reverse):
    L = g.shape[0]
    row = lax.broadcasted_iota(jnp.int32, (L, L), 0)
    col = lax.broadcasted_iota(jnp.int32, (L, L), 1)
    lane_i0 = MISC_GATE0 + (2 * d) * MLSTM_HEADS
    tri = jnp.where((col >= row) if reverse else (col <= row), 1.0, 0.0).astype(BF16)
    b_all = sum(_dot(tri, p) for p in _split3(_log_sigmoid(g)))
    pm_all = g - pltpu.roll(b_all, LANES - MLSTM_HEADS, axis=1)
    shift = 1
    while shift < L:
        if reverse:
            moved = jnp.where(row < L - shift, pltpu.roll(pm_all, L - shift, axis=0), NEG_BIG)
        else:
            moved = jnp.where(row >= shift, pltpu.roll(pm_all, shift, axis=0), NEG_BIG)
        pm_all = jnp.maximum(pm_all, moved)
        shift *= 2
    gt = g.T
    bt = b_all.T
    lane_rep = lambda x, lane: jnp.broadcast_to(x[:, lane:lane + 1], (L, LANES))
    heads = []
    for hd in range(MLSTM_HEADS):
        lane_i = lane_i0 + hd
        lane_f = lane_i0 + MLSTM_HEADS + hd
        heads.append((lane_rep(g, lane_i), lane_rep(b_all, lane_f), lane_rep(pm_all, lane_i),
                      gt[lane_i:lane_i + 1, :], bt[lane_f:lane_f + 1, :]))
    return heads


def _mlstm_kernel(qf_ref, kf_ref, vf_ref, gf_ref, qb_ref, kb_ref, vb_ref, gb_ref, bias_ref,
                  hf_ref, hb_ref, c_ref, m_ref):
    @pl.when(pl.program_id(0) == 0)
    def _():
        c_ref[...] = jnp.zeros_like(c_ref)
        m_ref[...] = jnp.zeros_like(m_ref)

    L = MLSTM_CHUNK
    n_sub = gf_ref.shape[0] // L
    dirs = ((qf_ref, kf_ref, vf_ref, gf_ref, hf_ref, False), (qb_ref, kb_ref, vb_ref, gb_ref, hb_ref, True))
    scan_order = lambda reverse: range(n_sub - 1, -1, -1) if reverse else range(n_sub)
    gates = {(d, sub): _mlstm_gates(g_ref[sub * L:(sub + 1) * L, :] + bias_ref[...], d, reverse)
             for d, (_, _, _, g_ref, _, reverse) in enumerate(dirs) for sub in scan_order(reverse)}
    outputs, states = [], []
    for d, (q_ref, k_ref, v_ref, g_ref, h_ref, reverse) in enumerate(dirs):
        for hd in range(MLSTM_HEADS):
            sl = slice(hd * MLSTM_HD, (hd + 1) * MLSTM_HD)
            idx = d * MLSTM_HEADS + hd
            c, m = c_ref[idx], m_ref[idx]
            for sub in scan_order(reverse):
                rows = slice(sub * L, (sub + 1) * L)
                h, c, m = _mlstm_chain(q_ref[rows, sl], k_ref[rows, sl], v_ref[rows, sl], *gates[d, sub][hd],
                                       c, m, reverse)
                outputs.append((h_ref, rows, sl, h))
            states.append((idx, c, m))
    for h_ref, rows, sl, h in outputs:
        h_ref[rows, sl] = h
    for idx, c, m in states:
        c_ref[idx] = c
        m_ref[idx] = m


def _mlstm(u, bias_misc):
    S = u.shape[0]
    L = MLSTM_STEP_CHUNKS * MLSTM_CHUNK
    nc = S // L
    fwd = lambda c: c
    bwd = lambda c: nc - 1 - c

    def spec(off, width, cmap):
        return pl.BlockSpec((L, width), lambda c: (cmap(c), off // width))

    in_specs = []
    for cmap in (fwd, bwd):
        in_specs += [spec(OFF_MQ, D_MLSTM, cmap), spec(OFF_MK, D_MLSTM, cmap), spec(OFF_MV, D_MLSTM, cmap),
                     spec(OFF_MISC, LANES, cmap)]
    in_specs.append(pl.BlockSpec((1, LANES), lambda c: (0, 0)))
    nstate = 2 * MLSTM_HEADS
    return pl.pallas_call(
        _mlstm_kernel,
        grid=(nc,),
        in_specs=in_specs,
        out_specs=[pl.BlockSpec((L, D_MLSTM), lambda c: (c, 0)),
                   pl.BlockSpec((L, D_MLSTM), lambda c: (nc - 1 - c, 0))],
        out_shape=[jax.ShapeDtypeStruct((S, D_MLSTM), F32)] * 2,
        scratch_shapes=[pltpu.VMEM((nstate, MLSTM_HD, MLSTM_HD + LANES), F32),
                        pltpu.VMEM((nstate, 1, LANES), F32)],
        compiler_params=_cparams(("arbitrary",)),
        name="mlstm",
    )(u, u, u, u, u, u, u, u, bias_misc)


def _mlstm_out_kernel(hf_ref, hb_ref, o_ref, z_ref, g_ref, out_ref):
    for hd in range(MLSTM_HEADS):
        sl = slice(hd * MLSTM_HD, (hd + 1) * MLSTM_HD)
        h = hf_ref[:, sl] + hb_ref[:, sl]
        ms = jnp.mean(h * h, axis=-1, keepdims=True)
        hn = h * lax.rsqrt(ms + NORM_EPS) * g_ref[:, sl]
        out_ref[:, sl] = (_sigmoid(o_ref[:, sl]) * hn * _silu(z_ref[:, sl])).astype(out_ref.dtype)


def _mlstm_out(hf, hb, u, norm_g, tm):
    S = hf.shape[0]
    return pl.pallas_call(
        _mlstm_out_kernel,
        grid=(S // tm,),
        in_specs=[
            pl.BlockSpec((tm, D_MLSTM), lambda i: (i, 0)),
            pl.BlockSpec((tm, D_MLSTM), lambda i: (i, 0)),
            pl.BlockSpec((tm, D_MLSTM), lambda i: (i, OFF_MO // D_MLSTM)),
            pl.BlockSpec((tm, D_MLSTM), lambda i: (i, OFF_MZ // D_MLSTM)),
            pl.BlockSpec((1, D_MLSTM), lambda i: (0, 0)),
        ],
        out_specs=pl.BlockSpec((tm, D_MLSTM), lambda i: (i, 0)),
        out_shape=jax.ShapeDtypeStruct((S, D_MLSTM), BF16),
        compiler_params=_cparams(("parallel",)),
        name="mlstm_out",
    )(hf, hb, u, u, norm_g)


def _rope128(x, g, cos_ref, sin_ref):
    lane = lax.broadcasted_iota(jnp.int32, x.shape, 1)
    x = jnp.where(lane < MLA_ROPE, x, 0.0)
    ms = jnp.sum(x * x, axis=-1, keepdims=True) * (1.0 / MLA_ROPE)
    xn = x * lax.rsqrt(ms + NORM_EPS) * g
    half = MLA_ROPE // 2
    partner = jnp.where(lane < half, pltpu.roll(xn, LANES - half, axis=1), pltpu.roll(xn, half, axis=1))
    return xn * cos_ref[...] + partner * sin_ref[...]


def _q_up_kernel(x_ref, g_ref, w_ref, qn_ref, qr_ref, cos_ref, sin_ref, o_ref, h_ref, acc_ref,
                 *, heads_per_tile, n_col, n_tiles):
    s = pl.program_id(0)

    @pl.when(jnp.logical_and(s % n_col == 0, s < n_tiles))
    def _():
        x = x_ref[...]
        ms = jnp.mean(x * x, axis=-1, keepdims=True)
        h_ref[...] = (x * lax.rsqrt(ms + NORM_EPS) * g_ref[...]).astype(BF16)

    @pl.when(s == 0)
    def _():
        acc_ref[1] = jnp.zeros(acc_ref.shape[1:], acc_ref.dtype)

    scale = MLA_QK ** -0.5 * LOG2_E

    def step(fill, drain):
        for hh in range(heads_per_tile):
            cols = slice(hh * QK_PAD, (hh + 1) * QK_PAD)
            acc_ref[fill, :, cols] = _dot(h_ref[...], w_ref[:, cols])
        for hh in range(heads_per_tile):
            nope = acc_ref[drain, :, hh * QK_PAD: hh * QK_PAD + MLA_NOPE]
            ms = jnp.mean(nope * nope, axis=-1, keepdims=True)
            nope = nope * lax.rsqrt(ms + NORM_EPS) * qn_ref[...]
            pe = _rope128(acc_ref[drain, :, hh * QK_PAD + MLA_NOPE: (hh + 1) * QK_PAD], qr_ref[...], cos_ref, sin_ref)
            o_ref[hh * QK_PAD: hh * QK_PAD + MLA_NOPE, :] = (nope * scale).astype(o_ref.dtype).T
            o_ref[hh * QK_PAD + MLA_NOPE: (hh + 1) * QK_PAD, :] = (pe * scale).astype(o_ref.dtype).T

    @pl.when(s % 2 == 0)
    def _():
        step(0, 1)

    @pl.when(s % 2 == 1)
    def _():
        step(1, 0)


def _q_up(u, g, w, layer, qn_g, qr_g, cos_t, sin_t, tm, heads_per_tile):
    S = u.shape[0]
    tn = heads_per_tile * QK_PAD
    N = MLA_HEADS * QK_PAD
    n_col = N // tn
    n_tiles = (S // tm) * n_col
    fill_row = lambda s: jnp.minimum(s, n_tiles - 1) // n_col
    fill_col = lambda s: jnp.minimum(s, n_tiles - 1) % n_col
    drain_row = lambda s: jnp.maximum(s - 1, 0) // n_col
    drain_col = lambda s: jnp.maximum(s - 1, 0) % n_col
    return pl.pallas_call(
        functools.partial(_q_up_kernel, heads_per_tile=heads_per_tile, n_col=n_col, n_tiles=n_tiles),
        grid=(n_tiles + 1,),
        in_specs=[
            pl.BlockSpec((tm, Q_LORA), lambda s: (fill_row(s), OFF_QLAT // Q_LORA)),
            pl.BlockSpec((1, Q_LORA), lambda s: (0, 0)),
            pl.BlockSpec((None, Q_LORA, tn), lambda s: (layer, 0, fill_col(s))),
            pl.BlockSpec((1, LANES), lambda s: (0, 0)),
            pl.BlockSpec((1, LANES), lambda s: (0, 0)),
            pl.BlockSpec((tm, LANES), lambda s: (drain_row(s), 0)),
            pl.BlockSpec((tm, LANES), lambda s: (drain_row(s), 0)),
        ],
        out_specs=pl.BlockSpec((tn, tm), lambda s: (drain_col(s), drain_row(s))),
        out_shape=jax.ShapeDtypeStruct((N, S), BF16),
        scratch_shapes=[pltpu.VMEM((tm, Q_LORA), BF16), pltpu.VMEM((2, tm, tn), F32)],
        compiler_params=_cparams(("arbitrary",)),
        name="q_up",
    )(u, g, w, qn_g, qr_g, cos_t, sin_t)


def _kv_up_kernel(x_ref, misc_ref, g_ref, w_ref, kn_ref, kr_ref, cos_ref, sin_ref, k_ref, vt_ref):
    x = x_ref[...]
    ms = jnp.mean(x * x, axis=-1, keepdims=True)
    h = (x * lax.rsqrt(ms + NORM_EPS) * g_ref[...]).astype(BF16)
    acc = _dot(h, w_ref[...])
    pe = _rope128(misc_ref[...], kr_ref[...], cos_ref, sin_ref).astype(k_ref.dtype)
    for hd in range(MLA_HEADS):
        nope = acc[:, hd * MLA_NOPE: (hd + 1) * MLA_NOPE]
        ms = jnp.mean(nope * nope, axis=-1, keepdims=True)
        k_ref[:, hd * QK_PAD: hd * QK_PAD + MLA_NOPE] = (nope * lax.rsqrt(ms + NORM_EPS) * kn_ref[...]).astype(k_ref.dtype)
        k_ref[:, hd * QK_PAD + MLA_NOPE: (hd + 1) * QK_PAD] = pe
        v = acc[:, MLA_HEADS * MLA_NOPE + hd * MLA_V: MLA_HEADS * MLA_NOPE + (hd + 1) * MLA_V]
        vt_ref[hd * V_AUG: hd * V_AUG + MLA_V, :] = v.astype(vt_ref.dtype).T
        vt_ref[hd * V_AUG + MLA_V: (hd + 1) * V_AUG, :] = jnp.ones((V_ONES, vt_ref.shape[1]), vt_ref.dtype)


def _kv_up(u, g, w, kn_g, kr_g, cos_t, sin_t, tk):
    S = u.shape[0]
    return pl.pallas_call(
        _kv_up_kernel,
        grid=(S // tk,),
        in_specs=[
            pl.BlockSpec((tk, KV_LORA), lambda i: (i, OFF_KVLAT // KV_LORA)),
            pl.BlockSpec((tk, LANES), lambda i: (i, OFF_MISC // LANES)),
            pl.BlockSpec((1, KV_LORA), lambda i: (0, 0)),
            pl.BlockSpec((KV_LORA, 2 * D_MLA), lambda i: (0, 0)),
            pl.BlockSpec((1, LANES), lambda i: (0, 0)),
            pl.BlockSpec((1, LANES), lambda i: (0, 0)),
            pl.BlockSpec((tk, LANES), lambda i: (i, 0)),
            pl.BlockSpec((tk, LANES), lambda i: (i, 0)),
        ],
        out_specs=[pl.BlockSpec((tk, MLA_HEADS * QK_PAD), lambda i: (i, 0)),
                   pl.BlockSpec((None, MLA_HEADS * V_AUG, tk), lambda i: (i, 0, 0))],
        out_shape=[jax.ShapeDtypeStruct((S, MLA_HEADS * QK_PAD), BF16),
                   jax.ShapeDtypeStruct((S // tk, MLA_HEADS * V_AUG, tk), BF16)],
        compiler_params=_cparams(("parallel",)),
        name="kv_up",
    )(u, u, g, w, kn_g, kr_g, cos_t, sin_t)


def _attn_kernel(qt_ref, qnext_ref, k_ref, vt_ref, z_ref, o_ref, s_ref, acc_ref, cmax_ref, fin_ref,
                 *, group_size, n_q, n_tiles):
    n_chunks, _, tk = vt_ref.shape
    tq = qt_ref.shape[1]
    n_slots = s_ref.shape[0]
    step = pl.program_id(0)
    q_tile = jnp.minimum(step, n_tiles - 1) % n_q

    def scores(q_ref, j, slot):
        k = k_ref[pl.ds(pl.multiple_of(j * tk, tk), tk), :]
        st = _dot(k, q_ref[...])
        s_ref[slot] = st
        return jnp.max(st, axis=0, keepdims=True)

    def accumulate(j, slot, m, cmax):
        m_new = jnp.maximum(m, cmax)
        alpha = jnp.exp2(m - m_new)
        p = jnp.exp2(s_ref[slot] - m_new).astype(BF16)
        acc_ref[...] = alpha * acc_ref[...] + _dot(vt_ref[j], p)
        return m_new

    def group(j0, m, cmax, last):
        for u in range(group_size):
            if last and u == group_size - 1:
                cnext = scores(qnext_ref, 0, 0)
            else:
                cnext = scores(qt_ref, j0 + u + 1, (u + 1) % n_slots)
            m = accumulate(j0 + u, u % n_slots, m, cmax)
            cmax = cnext
        return m, cmax

    @pl.when(step == 0)
    def _():
        fin_ref[...] = jnp.ones_like(fin_ref)

    @pl.when(q_tile == 0)
    def _():
        cmax_ref[...] = scores(qt_ref, 0, 0)

    o = (fin_ref[0:MLA_V, :] / fin_ref[MLA_V:MLA_V + 1, :]).T
    o_ref[...] = (o * _silu(z_ref[...])).astype(o_ref.dtype)

    acc_ref[...] = jnp.zeros_like(acc_ref)
    m = jnp.full((1, tq), NEG_BIG, F32)
    n_groups = n_chunks // group_size
    m, cmax = lax.fori_loop(0, n_groups - 1, lambda g, c: group(g * group_size, c[0], c[1], False),
                            (m, cmax_ref[...]))
    _, cmax_ref[...] = group((n_groups - 1) * group_size, m, cmax, True)
    fin_ref[...] = acc_ref[...]


def _attn(qt, kf, vt, u, tq, group_size, n_slots):
    S = kf.shape[0]
    n_chunks, _, tk = vt.shape
    n_q = S // tq
    n_tiles = MLA_HEADS * n_q
    assert n_chunks % group_size == 0 and group_size % n_slots == 0 and n_slots >= 2
    tile = lambda s: jnp.minimum(s, n_tiles - 1)
    done = lambda s: jnp.maximum(s - 1, 0)
    return pl.pallas_call(
        functools.partial(_attn_kernel, group_size=group_size, n_q=n_q, n_tiles=n_tiles),
        grid=(n_tiles + 1,),
        in_specs=[
            pl.BlockSpec((QK_PAD, tq), lambda s: (tile(s) // n_q, tile(s) % n_q)),
            pl.BlockSpec((QK_PAD, tq), lambda s: (tile(s) // n_q, jnp.minimum(tile(s) % n_q + 1, n_q - 1))),
            pl.BlockSpec((S, QK_PAD), lambda s: (0, tile(s) // n_q)),
            pl.BlockSpec((n_chunks, V_AUG, tk), lambda s: (0, tile(s) // n_q, 0)),
            pl.BlockSpec((tq, MLA_V), lambda s: (done(s) % n_q, OFF_AZ // MLA_V + done(s) // n_q)),
        ],
        out_specs=pl.BlockSpec((tq, MLA_V), lambda s: (done(s) % n_q, done(s) // n_q)),
        out_shape=jax.ShapeDtypeStruct((S, D_MLA), BF16),
        scratch_shapes=[pltpu.VMEM((n_slots, tk, tq), F32), pltpu.VMEM((V_AUG, tq), F32),
                        pltpu.VMEM((1, tq), F32), pltpu.VMEM((V_AUG, tq), F32)],
        compiler_params=_cparams(("arbitrary",)),
        name="attn",
    )(qt, qt, kf, vt, u)


def _out_proj_kernel(p_ref, m_ref, a_ref, w_ref, x_ref, o_ref):
    acc = _dot(p_ref[...], w_ref[0:D_POOL, :])
    acc += _dot(m_ref[...], w_ref[D_POOL:D_POOL + D_MLSTM, :])
    acc += _dot(a_ref[...], w_ref[D_POOL + D_MLSTM:, :])
    o_ref[...] = x_ref[...] + acc


def _out_proj(pool_o, mlstm_o, mla_o, w, layer, x, tm, tn):
    S = x.shape[0]
    return pl.pallas_call(
        _out_proj_kernel,
        grid=(S // tm, D_MODEL // tn),
        in_specs=[
            pl.BlockSpec((tm, D_POOL), lambda i, j: (i, 0)),
            pl.BlockSpec((tm, D_MLSTM), lambda i, j: (i, 0)),
            pl.BlockSpec((tm, D_MLA), lambda i, j: (i, 0)),
            pl.BlockSpec((None, D_MODEL, tn), lambda i, j: (layer, 0, j)),
            pl.BlockSpec((tm, tn), lambda i, j: (i, j)),
        ],
        out_specs=pl.BlockSpec((tm, tn), lambda i, j: (i, j)),
        out_shape=jax.ShapeDtypeStruct((S, D_MODEL), F32),
        compiler_params=_cparams(("parallel", "arbitrary")),
        name="out_proj",
    )(pool_o, mlstm_o, mla_o, w, x)


def _pad_lanes(v, offset=0):
    return jnp.zeros((1, LANES), F32).at[0, offset:offset + v.shape[0]].set(v.astype(F32))


_SRC_SIZES = (D_POOL, D_POOL, D_MLSTM, D_MLSTM, D_MLSTM, D_MLSTM, D_MLSTM, MLSTM_GATES,
              Q_LORA, KV_LORA, MLA_ROPE, D_MLA)
(_SRC_PX, _, _, _, _, _, _, _SRC_MG, _SRC_QLAT, _SRC_KVLAT, _SRC_KROPE, _SRC_AZ) = (
    int(o) for o in np.concatenate([[0], np.cumsum(_SRC_SIZES)])[:-1])
W_IN_MOVES = (
    (OFF_QLAT, _SRC_QLAT, Q_LORA), (OFF_KVLAT, _SRC_KVLAT, KV_LORA), (OFF_AZ, _SRC_AZ, D_MLA),
    (OFF_PX, _SRC_PX, 2 * D_POOL + 5 * D_MLSTM),
)
W_IN_TILE = 256
W_IN_MISC_TILE = OFF_MISC // W_IN_TILE


def _w_in_src_row(j):
    unit = 16
    row = jnp.int32(_SRC_KROPE // unit)
    for dst, src, width in reversed(W_IN_MOVES):
        row = jnp.where(j < (dst + width) // W_IN_TILE, src // unit + (j - dst // W_IN_TILE) * (W_IN_TILE // unit), row)
    return row * unit


def _w_in_prep_kernel(rows_ref, gate_rows_ref, o_ref):
    j = pl.program_id(1)

    @pl.when(j != W_IN_MISC_TILE)
    def _():
        o_ref[...] = rows_ref[0].T.astype(o_ref.dtype)

    @pl.when(j == W_IN_MISC_TILE)
    def _():
        pad = jnp.zeros((W_IN_TILE - MLA_ROPE - MLSTM_GATES, rows_ref.shape[2]), rows_ref.dtype)
        tile = jnp.concatenate([rows_ref[0, 0:MLA_ROPE, :], gate_rows_ref[0, 0:MLSTM_GATES, :], pad], axis=0)
        o_ref[...] = tile.T.astype(o_ref.dtype)


def _prep_w_in(w_in):
    wt = jnp.swapaxes(w_in, 1, 2)
    n_layers, _, K = wt.shape
    elem = pl.Element
    return pl.pallas_call(
        _w_in_prep_kernel,
        grid=(n_layers, N_IN_PAD // W_IN_TILE),
        in_specs=[pl.BlockSpec((elem(1), elem(W_IN_TILE), elem(K)), lambda l, j: (l, _w_in_src_row(j), 0)),
                  pl.BlockSpec((elem(1), elem(W_IN_TILE), elem(K)), lambda l, j: (l, _SRC_MG, 0))],
        out_specs=pl.BlockSpec((None, K, W_IN_TILE), lambda l, j: (l, 0, j)),
        out_shape=jax.ShapeDtypeStruct((n_layers, K, N_IN_PAD), BF16),
        compiler_params=_cparams(("parallel", "parallel")),
        name="w_in_prep",
    )(wt, wt)


def _w_uq_prep_kernel(w_ref, o_ref):
    for hd in range(MLA_HEADS):
        o_ref[:, hd * QK_PAD: hd * QK_PAD + MLA_QK] = w_ref[:, hd * MLA_QK: (hd + 1) * MLA_QK].astype(o_ref.dtype)
        o_ref[:, hd * QK_PAD + MLA_QK: (hd + 1) * QK_PAD] = jnp.zeros((o_ref.shape[0], QK_PAD - MLA_QK), o_ref.dtype)


def _prep_weight(body, w, n_out, tr, name):
    n_layers, K, N = w.shape
    return pl.pallas_call(
        body,
        grid=(n_layers, K // tr),
        in_specs=[pl.BlockSpec((None, tr, N), lambda l, i: (l, i, 0))],
        out_specs=pl.BlockSpec((None, tr, n_out), lambda l, i: (l, i, 0)),
        out_shape=jax.ShapeDtypeStruct((n_layers, K, n_out), BF16),
        compiler_params=_cparams(("parallel", "parallel")),
        name=name,
    )(w)


def _prep_layer(norm_g, gate_bias, pool_w, pool_scale, mlstm_norm_g, qlat_g, kvlat_g, w_ukv, qn_g, qr_g, kn_g, kr_g):
    wkv = w_ukv.reshape(KV_LORA, MLA_HEADS, MLA_NOPE + MLA_V)
    w_ukv_p = jnp.concatenate([wkv[:, :, :MLA_NOPE].reshape(KV_LORA, -1),
                               wkv[:, :, MLA_NOPE:].reshape(KV_LORA, -1)], axis=1).astype(BF16)
    return dict(
        norm_g=norm_g.reshape(1, D_MODEL),
        bias_misc=_pad_lanes(gate_bias, MISC_GATE0),
        pool_w=pool_w.astype(BF16), pool_scale=pool_scale.reshape(1, D_POOL),
        mlstm_norm_g=mlstm_norm_g.reshape(1, D_MLSTM),
        qlat_g=qlat_g.reshape(1, Q_LORA),
        kvlat_g=kvlat_g.reshape(1, KV_LORA), w_ukv=w_ukv_p,
        qn_g=qn_g.reshape(1, MLA_NOPE), qr_g=_pad_lanes(qr_g),
        kn_g=kn_g.reshape(1, MLA_NOPE), kr_g=_pad_lanes(kr_g),
    )


def _rope_tables(S):
    pos = jnp.arange(S, dtype=F32)
    inv_freq = ROPE_THETA ** (-(jnp.arange(0, MLA_ROPE, 2, dtype=F32) / MLA_ROPE))
    ang = pos[:, None] * inv_freq[None, :]
    cos, sin = jnp.cos(ang), jnp.sin(ang)
    pad = jnp.zeros((S, LANES - MLA_ROPE), F32)
    return jnp.concatenate([cos, cos, pad], axis=1), jnp.concatenate([-sin, sin, pad], axis=1)


ATTN_SLOTS = 2


def _tiles(S):
    t = lambda pref: min(pref, S)
    n_chunks = S // t(512)
    slots = min(ATTN_SLOTS, n_chunks)
    return dict(tm_in=t(512), tn_in=768, tm_pool=t(1024), tm_mo=t(1024), tm_q=t(512), q_heads=4,
                tk=t(512), tq=t(512), attn_group=min(16, n_chunks), attn_slots=slots, tm_out=t(512), tn_out=1024)


def _layer(x, layer, p, big, cos_t, sin_t):
    S = x.shape[0]
    t = _tiles(S)
    u = _in_proj(x, p["norm_g"], big["w_in"], layer, t["tm_in"], t["tn_in"])
    pool_o = _pool(u, p["pool_w"], p["pool_scale"], t["tm_pool"])
    hf, hb = _mlstm(u, p["bias_misc"])
    mlstm_o = _mlstm_out(hf, hb, u, p["mlstm_norm_g"], t["tm_mo"])
    qt = _q_up(u, p["qlat_g"], big["w_uq"], layer, p["qn_g"], p["qr_g"], cos_t, sin_t, t["tm_q"], t["q_heads"])
    kf, vt = _kv_up(u, p["kvlat_g"], p["w_ukv"], p["kn_g"], p["kr_g"], cos_t, sin_t, t["tk"])
    mla_o = _attn(qt, kf, vt, u, t["tq"], t["attn_group"], t["attn_slots"])
    return _out_proj(pool_o, mlstm_o, mla_o, big["w_out"], layer, x, t["tm_out"], t["tn_out"])


def _trunk(x, layers, big, cos_t, sin_t):
    x = x[0]
    for layer, p in enumerate(layers):
        x = _layer(x, layer, p, big, cos_t, sin_t)
    return x[None]


def kernel(x_prompt, x_sample, norm_g, w_in, gate_bias, pool_w, pool_scale, mlstm_norm_g,
           qlat_g, w_uq, kvlat_g, w_ukv, qn_g, qr_g, kn_g, kr_g, w_out):
    small = (norm_g, gate_bias, pool_w, pool_scale, mlstm_norm_g, qlat_g, kvlat_g, w_ukv, qn_g, qr_g, kn_g, kr_g)
    layers = [_prep_layer(*[w[l] for w in small]) for l in range(norm_g.shape[0])]
    big = dict(w_in=_prep_w_in(w_in),
               w_uq=_prep_weight(_w_uq_prep_kernel, w_uq, MLA_HEADS * QK_PAD, 512, "w_uq_prep"),
               w_out=w_out.astype(BF16))
    cos_t, sin_t = _rope_tables(max(x_prompt.shape[1], x_sample.shape[1]))
    return (_trunk(x_prompt, layers, big, cos_t, sin_t), _trunk(x_sample, layers, big, cos_t, sin_t))
```
